```python
import jax, jax.numpy as jnp
from jax import lax
import numpy as np

D_MODEL = 1024
BATCH = 8
SEQ = 4096
DEPTH = 2

HEAD_DIM = 64
SWA_Q_HEADS = 8
SWA_KV_HEADS = 2
SWA_G = SWA_Q_HEADS // SWA_KV_HEADS
DSA_Q_HEADS = 8
DSA_KV_HEADS = 1
DSA_G = DSA_Q_HEADS // DSA_KV_HEADS
IDX_HEADS = 8
IDX_DIM = 32
TOPK_MAX = 256
WINDOW = 128
BLOCK = 128
ROPE_THETA = 10000.0
PLE_DIM = 256
MIX_WIDTH = (SWA_Q_HEADS + DSA_Q_HEADS) * HEAD_DIM
D_FF = -(-(8 * D_MODEL) // (3 * 256)) * 256
DN_ALPHA = (2 * DEPTH) ** 0.25
DN_BETA = (8 * DEPTH) ** -0.25
LN_EPS = 1e-5
NEG = -1e30
IN_SIZES = (SWA_Q_HEADS * HEAD_DIM, SWA_KV_HEADS * HEAD_DIM, SWA_KV_HEADS * HEAD_DIM,
            DSA_Q_HEADS * HEAD_DIM, DSA_KV_HEADS * HEAD_DIM, DSA_KV_HEADS * HEAD_DIM,
            IDX_HEADS * IDX_DIM, IDX_DIM, IDX_HEADS)
IN_COLS = sum(IN_SIZES)

kernel_name = "hymba_swa_sink_dsa_deepnorm_ple"


def layer_norm(x, g, b):
    xf = x.astype(jnp.float32)
    mu = xf.mean(-1, keepdims=True)
    var = jnp.square(xf - mu).mean(-1, keepdims=True)
    y = (xf - mu) * lax.rsqrt(var + LN_EPS)
    return (y * g.astype(jnp.float32) + b.astype(jnp.float32)).astype(x.dtype)


def rope(x, positions):
    half = x.shape[-1] // 2
    inv = ROPE_THETA ** (-jnp.arange(half, dtype=jnp.float32) / half)
    ang = positions.astype(jnp.float32)[..., None] * inv
    ang = ang.reshape(ang.shape[:2] + (1,) * (x.ndim - 3) + (half,))
    cos, sin = jnp.cos(ang), jnp.sin(ang)
    x1 = x[..., :half].astype(jnp.float32)
    x2 = x[..., half:].astype(jnp.float32)
    return jnp.concatenate([x1 * cos - x2 * sin, x2 * cos + x1 * sin], -1).astype(x.dtype)


def swa_sink_attention(q, k, v, sinks):
    B, S, Hkv, G, D = q.shape
    nb = S // BLOCK
    qb = q.reshape(B, nb, BLOCK, Hkv, G, D)
    kb = k.reshape(B, nb, BLOCK, Hkv, D)
    vb = v.reshape(B, nb, BLOCK, Hkv, D)
    prev = lambda t: jnp.concatenate([jnp.zeros_like(t[:, :1]), t[:, :-1]], axis=1)
    kk = jnp.concatenate([prev(kb), kb], axis=2)
    vv = jnp.concatenate([prev(vb), vb], axis=2)
    s = jnp.einsum('bnqhgd,bnkhd->bnhgqk', qb, kk).astype(jnp.float32) * (D ** -0.5)
    i = jnp.arange(BLOCK)[:, None]
    j = jnp.arange(2 * BLOCK)[None, :]
    diff = BLOCK + i - j
    band = (diff >= 0) & (diff < WINDOW)
    not_pad = (jnp.arange(nb)[:, None, None] > 0) | (j[None] >= BLOCK)
    mask = band[None] & not_pad
    s = jnp.where(mask[None, :, None, None], s, NEG)
    sink = sinks.astype(jnp.float32)[None, None, :, :, None, None]
    m = jnp.maximum(s.max(-1, keepdims=True), sink)
    pr = jnp.exp(s - m)
    denom = pr.sum(-1, keepdims=True) + jnp.exp(sink - m)
    o = jnp.einsum('bnhgqk,bnkhd->bnqhgd', (pr / denom).astype(v.dtype), vv)
    return o.reshape(B, S, Hkv * G * D)


def dsa_sparse_attention(q, k, v, qi, ki, wi, k_top):
    B, S, Hkv, G, D = q.shape
    nb = S // BLOCK
    to_blocks = lambda t: jnp.moveaxis(t.reshape((B, nb, BLOCK) + t.shape[2:]), 1, 0)
    gather = jax.vmap(lambda kv, ix: kv[ix])
    key_pos = jnp.arange(S)

    def one_block(args):
        n, qb, qib, wib = args
        t = n * BLOCK + jnp.arange(BLOCK)
        sc = jnp.einsum('bqhd,bsd->bqhs', qib, ki).astype(jnp.float32)
        idx_score = jnp.einsum('bqhs,bqh->bqs', jax.nn.relu(sc), wib.astype(jnp.float32))
        causal = key_pos[None, :] <= t[:, None]
        idx_score = jnp.where(causal[None], idx_score, NEG)
        _, sel = lax.top_k(idx_score, k_top)
        ks = gather(k, sel)
        vs = gather(v, sel)
        s = jnp.einsum('bqhgd,bqjhd->bqhgj', qb, ks).astype(jnp.float32) * (D ** -0.5)
        valid = sel <= t[None, :, None]
        s = jnp.where(valid[:, :, None, None, :], s, NEG)
        pr = jax.nn.softmax(s, axis=-1)
        o = jnp.einsum('bqhgj,bqjhd->bqhgd', pr.astype(v.dtype), vs)
        return o.reshape(B, BLOCK, Hkv * G * D)

    out = lax.map(one_block, (jnp.arange(nb), to_blocks(q), to_blocks(qi), to_blocks(wi)))
    return jnp.moveaxis(out, 0, 1).reshape(B, S, Hkv * G * D)


def hybrid_layer(x, p_i, positions, w_in, sinks, idx_k_g, idx_k_b, w_o, ln1_g, ln1_b,
                 w_gu, w_down, w_pg, w_pp, ln2_g, ln2_b):
    B, S, _ = x.shape
    k_top = min(TOPK_MAX, S // 4)
    h = x @ w_in
    cuts = [int(c) for c in np.cumsum(IN_SIZES)[:-1]]
    qa, ka, va, qb, kb, vb, qi, ki, wi = jnp.split(h, cuts, axis=-1)
    qa = rope(qa.reshape(B, S, SWA_KV_HEADS, SWA_G, HEAD_DIM), positions)
    ka = rope(ka.reshape(B, S, SWA_KV_HEADS, HEAD_DIM), positions)
    va = va.reshape(B, S, SWA_KV_HEADS, HEAD_DIM)
    qb = rope(qb.reshape(B, S, DSA_KV_HEADS, DSA_G, HEAD_DIM), positions)
    kb = rope(kb.reshape(B, S, DSA_KV_HEADS, HEAD_DIM), positions)
    vb = vb.reshape(B, S, DSA_KV_HEADS, HEAD_DIM)
    qi = rope(qi.reshape(B, S, IDX_HEADS, IDX_DIM), positions)
    ki = rope(layer_norm(ki, idx_k_g, idx_k_b), positions)
    wi = wi * (IDX_HEADS ** -0.5 * IDX_DIM ** -0.5)
    oa = swa_sink_attention(qa, ka, va, sinks.reshape(SWA_KV_HEADS, SWA_G))
    ob = dsa_sparse_attention(qb, kb, vb, qi, ki, wi, k_top)
    mix = jnp.concatenate([oa, ob], axis=-1) @ w_o
    x = layer_norm(DN_ALPHA * x + mix, ln1_g, ln1_b)
    g, u = jnp.split(x @ w_gu, 2, axis=-1)
    ffn = (jax.nn.silu(g) * u) @ w_down
    ple = (p_i @ w_pp) * jax.nn.sigmoid(x @ w_pg)
    return layer_norm(DN_ALPHA * x + ffn + ple, ln2_g, ln2_b)


def setup_inputs(seed: int = 0) -> dict:
    key = jax.random.key(seed)
    ks = jax.random.split(key, 20)
    nrm = lambda k, shape, scale: jax.random.normal(k, shape, jnp.float32) * scale
    L = DEPTH
    offs = jax.random.randint(ks[2], (BATCH, 1), 0, 1024)
    positions = (offs + jnp.arange(SEQ)[None, :]).astype(jnp.int32)
    return {
        "x": nrm(ks[0], (BATCH, SEQ, D_MODEL), 1.0),
        "p": nrm(ks[1], (DEPTH, BATCH, SEQ, PLE_DIM), 1.0),
        "positions": positions,
        "ln_in_g": 1.0 + nrm(ks[3], (D_MODEL,), 0.02),
        "ln_in_b": nrm(ks[4], (D_MODEL,), 0.02),
        "w_in": nrm(ks[5], (L, D_MODEL, IN_COLS), D_MODEL ** -0.5),
        "attn_sinks": nrm(ks[6], (L, SWA_Q_HEADS), 0.5),
        "idx_k_g": 1.0 + nrm(ks[7], (L, IDX_DIM), 0.02),
        "idx_k_b": nrm(ks[8], (L, IDX_DIM), 0.02),
        "w_o": nrm(ks[9], (L, MIX_WIDTH, D_MODEL), DN_BETA * MIX_WIDTH ** -0.5),
        "ln1_g": 1.0 + nrm(ks[10], (L, D_MODEL), 0.02),
        "ln1_b": nrm(ks[11], (L, D_MODEL), 0.02),
        "w_gu": nrm(ks[12], (L, D_MODEL, 2 * D_FF), D_MODEL ** -0.5),
        "w_down": nrm(ks[13], (L, D_FF, D_MODEL), DN_BETA * D_FF ** -0.5),
        "w_pg": nrm(ks[14], (L, D_MODEL, D_MODEL), D_MODEL ** -0.5),
        "w_pp": nrm(ks[15], (L, PLE_DIM, D_MODEL), DN_BETA * PLE_DIM ** -0.5),
        "ln2_g": 1.0 + nrm(ks[16], (L, D_MODEL), 0.02),
        "ln2_b": nrm(ks[17], (L, D_MODEL), 0.02),
    }


def reference(x, p, positions, ln_in_g, ln_in_b, w_in, attn_sinks, idx_k_g, idx_k_b, w_o,
              ln1_g, ln1_b, w_gu, w_down, w_pg, w_pp, ln2_g, ln2_b):
    x = layer_norm(x, ln_in_g, ln_in_b)
    for i in range(DEPTH):
        x = hybrid_layer(x, p[i], positions, w_in[i], attn_sinks[i], idx_k_g[i], idx_k_b[i],
                         w_o[i], ln1_g[i], ln1_b[i], w_gu[i], w_down[i], w_pg[i], w_pp[i],
                         ln2_g[i], ln2_b[i])
    return x
```

```python
import functools

import jax
import jax.numpy as jnp
from jax import lax
from jax.experimental import pallas as pl
from jax.experimental.pallas import tpu as pltpu

D_MODEL = 1024
HEAD_DIM = 64
SWA_Q_HEADS = 8
SWA_KV_HEADS = 2
DSA_Q_HEADS = 8
IDX_HEADS = 8
IDX_DIM = 32
TOPK_MAX = 256
WINDOW = 128
BLOCK = 128
ROPE_THETA = 10000.0
PLE_DIM = 256
LN_EPS = 1e-5
NEG = -1e30
LANES = 128
QA_W = SWA_Q_HEADS * HEAD_DIM
QB_W = DSA_Q_HEADS * HEAD_DIM
QI_W = IDX_HEADS * IDX_DIM
IN_SIZES = (QA_W, SWA_KV_HEADS * HEAD_DIM, SWA_KV_HEADS * HEAD_DIM, QB_W, HEAD_DIM, HEAD_DIM,
            QI_W, IDX_DIM, IDX_HEADS)
EXT_SEGS = (("qa", QA_W), ("ka", LANES), ("va", LANES), ("qb", QB_W), ("kb", LANES),
            ("vlo", LANES), ("vhi", LANES), ("qi", QI_W), ("ki", LANES), ("wi", LANES))
EXT_W = sum(w for _, w in EXT_SEGS)
CK = 256
INT_MIN = -2 ** 31
INT_MAX = 2 ** 31 - 1
VMEM_LIMIT = 56 * 1024 * 1024

F32 = jnp.float32
BF16 = jnp.bfloat16
NT_DIMS = (((1,), (1,)), ((), ()))


def _ln(x, g, b):
    mu = jnp.mean(x, axis=-1, keepdims=True)
    xc = x - mu
    var = jnp.mean(xc * xc, axis=-1, keepdims=True)
    return xc * lax.rsqrt(var + LN_EPS) * g + b


def _rope_table_kernel(pos_ref, inv_ref, cos64_ref, sin64_ref, cos32_ref, sin32_ref):
    ang = pos_ref[...].astype(F32) * inv_ref[...]
    lane = lax.broadcasted_iota(jnp.int32, ang.shape, 1)
    c, s = jnp.cos(ang), jnp.sin(ang)

    def tile64(a):
        a = jnp.where(lane < 32, a, pltpu.roll(a, 32, 1))
        return jnp.where(lane < 64, a, pltpu.roll(a, 64, 1))

    def tile32(a):
        a = pltpu.roll(a, 96, 1)
        a = jnp.where(lane < 16, a, pltpu.roll(a, 16, 1))
        a = jnp.where(lane < 32, a, pltpu.roll(a, 32, 1))
        return jnp.where(lane < 64, a, pltpu.roll(a, 64, 1))

    cos64_ref[...] = tile64(c)
    sin64_ref[...] = jnp.where(lane % 64 < 32, -1.0, 1.0) * tile64(s)
    cos32_ref[...] = tile32(c)
    sin32_ref[...] = jnp.where(lane % 32 < 16, -1.0, 1.0) * tile32(s)


def _rope_tables(pos_col, inv_row):
    t = pos_col.shape[0]
    tm = 1024
    row = pl.BlockSpec((tm, LANES), lambda i: (i, 0))
    return pl.pallas_call(
        _rope_table_kernel,
        grid=(t // tm,),
        in_specs=[pl.BlockSpec((tm, 1), lambda i: (i, 0)),
                  pl.BlockSpec((1, LANES), lambda i: (0, 0))],
        out_specs=[row] * 4,
        out_shape=[jax.ShapeDtypeStruct((t, LANES), F32)] * 4,
        name="rope_tables",
    )(pos_col, inv_row)


def _rope_apply(h, cos, sin_signed, half):
    lane = lax.broadcasted_iota(jnp.int32, h.shape, 1)
    partner = jnp.where(lane % (2 * half) < half,
                        pltpu.roll(h, LANES - half, 1), pltpu.roll(h, half, 1))
    return h * cos + partner * sin_signed


def _in_proj_kernel(apply_ln, *refs):
    if apply_ln:
        (x_ref, lng_ref, lnb_ref, w_ref, cos64_ref, sin64_ref, cos32_ref, sin32_ref,
         kig_ref, kib_ref, xn_ref, *outs) = refs
        xn = _ln(x_ref[...], lng_ref[...], lnb_ref[...])
        xn_ref[...] = xn
        xb = xn.astype(BF16)
    else:
        (x_ref, w_ref, cos64_ref, sin64_ref, cos32_ref, sin32_ref,
         kig_ref, kib_ref, *outs) = refs
        xb = x_ref[...]
    out = dict(zip([n for n, _ in EXT_SEGS], outs))
    cos64, sin64 = cos64_ref[...], sin64_ref[...]
    cos32, sin32 = cos32_ref[...], sin32_ref[...]
    chunks = [(name, j) for name, width in EXT_SEGS for j in range(width // LANES)]
    mxu_n = 2 * LANES
    for c0 in range(0, len(chunks), mxu_n // LANES):
        group = chunks[c0:c0 + mxu_n // LANES]
        hh = jnp.dot(xb, w_ref[:, c0 * LANES:(c0 + len(group)) * LANES], preferred_element_type=F32)
        for k, (name, j) in enumerate(group):
            h = hh[:, k * LANES:(k + 1) * LANES]
            if name in ("qa", "qb"):
                h = _rope_apply(h, cos64, sin64, HEAD_DIM // 2) * (HEAD_DIM ** -0.5)
            elif name in ("ka", "kb"):
                h = _rope_apply(h, cos64, sin64, HEAD_DIM // 2)
            elif name == "qi":
                h = _rope_apply(h, cos32, sin32, IDX_DIM // 2)
            elif name == "ki":
                h = _rope_apply(_ln(h, kig_ref[...], kib_ref[...]), cos32, sin32, IDX_DIM // 2)
            elif name == "wi":
                h = h * (IDX_HEADS ** -0.5 * IDX_DIM ** -0.5)
            out[name][:, j * LANES:(j + 1) * LANES] = h.astype(out[name].dtype)


def _in_proj(x, w_ext, tables, kig, kib, ln=None):
    t = x.shape[0]
    tm = 512
    row = lambda w: pl.BlockSpec((tm, w), lambda i: (i, 0))
    const = lambda shape: pl.BlockSpec(shape, lambda i: (0, 0))
    in_specs = [row(D_MODEL)]
    args = [x]
    if ln is not None:
        in_specs += [const((1, D_MODEL))] * 2
        args += list(ln)
    in_specs += [const((D_MODEL, EXT_W))] + [row(LANES)] * 4 + [const((1, LANES))] * 2
    args += [w_ext, *tables, kig, kib]
    out_specs, out_shape = [], []
    if ln is not None:
        out_specs.append(row(D_MODEL))
        out_shape.append(jax.ShapeDtypeStruct((t, D_MODEL), F32))
    for name, width in EXT_SEGS:
        out_specs.append(row(width))
        out_shape.append(jax.ShapeDtypeStruct((t, width), F32 if name == "wi" else BF16))
    return pl.pallas_call(
        functools.partial(_in_proj_kernel, ln is not None),
        grid=(t // tm,),
        in_specs=in_specs, out_specs=out_specs, out_shape=out_shape,
        compiler_params=pltpu.CompilerParams(vmem_limit_bytes=VMEM_LIMIT),
        name="in_proj",
    )(*args)


def _swa_kernel(q_ref, kc_ref, kp_ref, vc_ref, vp_ref, snk1_ref, snk2_ref, o_ref):
    n = pl.program_id(1)
    lane = lax.broadcasted_iota(jnp.int32, (BLOCK, LANES), 1)
    lo = lane < HEAD_DIM
    q = q_ref[0].astype(F32)
    qc = [q[:, j * LANES:(j + 1) * LANES] for j in range(4)]
    keep_lo = lambda a: jnp.where(lo, a, 0.0)
    keep_hi = lambda a: jnp.where(lo, 0.0, a)
    lhs1 = jnp.concatenate([keep_lo(qc[0]), keep_lo(qc[1]), keep_hi(qc[2]), keep_hi(qc[3])], 0)
    lhs2 = jnp.concatenate([keep_hi(qc[0]), keep_hi(qc[1]), keep_lo(qc[2]), keep_lo(qc[3])], 0)
    kk = jnp.concatenate([kp_ref[0], kc_ref[0]], 0).astype(F32)
    kr = pltpu.roll(kk, HEAD_DIM, 1)
    s1 = lax.dot_general(lhs1.astype(BF16), kk.astype(BF16), NT_DIMS, preferred_element_type=F32)
    s2 = lax.dot_general(lhs2.astype(BF16), kr.astype(BF16), NT_DIMS, preferred_element_type=F32)
    i = lax.broadcasted_iota(jnp.int32, (4 * BLOCK, 2 * BLOCK), 0) % BLOCK
    j = lax.broadcasted_iota(jnp.int32, (4 * BLOCK, 2 * BLOCK), 1)
    diff = BLOCK + i - j
    first_key = jnp.where(n > 0, 0, BLOCK)
    mask = (diff >= 0) & (diff < WINDOW) & (j >= first_key)

    def softmax(s, sink):
        s = jnp.where(mask, s, NEG)
        m = jnp.maximum(jnp.max(s, axis=-1, keepdims=True), sink)
        p = jnp.exp(s - m)
        denom = jnp.sum(p, axis=-1, keepdims=True) + jnp.exp(sink - m)
        return p.astype(BF16), 1.0 / denom

    p1, r1 = softmax(s1, snk1_ref[...])
    p2, r2 = softmax(s2, snk2_ref[...])
    vv = jnp.concatenate([vp_ref[0], vc_ref[0]], 0).astype(F32)
    vr = pltpu.roll(vv, HEAD_DIM, 1)
    lo2 = jnp.concatenate([lo, lo], 0)
    v_lo = lambda a: jnp.where(lo2, a, 0.0).astype(BF16)
    v_hi = lambda a: jnp.where(lo2, 0.0, a).astype(BF16)
    h = 2 * BLOCK
    dot = functools.partial(jnp.dot, preferred_element_type=F32)
    o01 = dot(p1[:h], v_lo(vv)) + dot(p2[:h], v_hi(vr))
    o23 = dot(p2[h:], v_lo(vr)) + dot(p1[h:], v_hi(vv))
    o01 = o01 * jnp.where(lo2, r1[:h], r2[:h])
    o23 = o23 * jnp.where(lo2, r2[h:], r1[h:])
    o_ref[0, :, 0 * LANES:1 * LANES] = o01[:BLOCK].astype(o_ref.dtype)
    o_ref[0, :, 1 * LANES:2 * LANES] = o01[BLOCK:].astype(o_ref.dtype)
    o_ref[0, :, 2 * LANES:3 * LANES] = o23[:BLOCK].astype(o_ref.dtype)
    o_ref[0, :, 3 * LANES:4 * LANES] = o23[BLOCK:].astype(o_ref.dtype)


def _swa(qa, ka, va, snk1, snk2):
    b, s, _ = qa.shape
    cur = lambda w: pl.BlockSpec((1, BLOCK, w), lambda bi, n: (bi, n, 0))
    prev = lambda w: pl.BlockSpec((1, BLOCK, w), lambda bi, n: (bi, jnp.maximum(n - 1, 0), 0))
    snk = pl.BlockSpec((4 * BLOCK, 1), lambda bi, n: (0, 0))
    return pl.pallas_call(
        _swa_kernel,
        grid=(b, s // BLOCK),
        in_specs=[cur(QA_W), cur(LANES), prev(LANES), cur(LANES), prev(LANES), snk, snk],
        out_specs=cur(QA_W),
        out_shape=jax.ShapeDtypeStruct((b, s, QA_W), BF16),
        name="swa_attn",
    )(qa, ka, ka, va, va, snk1, snk2)


def _dsa_kernel(qi_ref, wi_ref, qb_ref, ki_ref, kb_ref, vlo_ref, vhi_ref, o_ref,
                qim_ref, wb_ref, qs_ref, key_ref, sel_ref, s_ref, macc_ref, mb_ref,
                lacc_ref, oacc_ref):
    n = pl.program_id(1)
    nck = n // (CK // BLOCK) + 1
    lane = lax.broadcasted_iota(jnp.int32, (BLOCK, LANES), 1)
    chunk_rows = lambda c: pl.ds(pl.multiple_of(c * CK, CK), CK)
    key_pos = lambda c: c * CK + lax.broadcasted_iota(jnp.int32, (BLOCK, CK), 1)
    causal = lambda c: key_pos(c) <= n * BLOCK + lax.broadcasted_iota(jnp.int32, (BLOCK, CK), 0)

    qi = qi_ref[0].astype(F32)
    wi = wi_ref[0]
    for h in range(IDX_HEADS):
        per = LANES // IDX_DIM
        c = qi[:, (h // per) * LANES:(h // per + 1) * LANES]
        qim_ref[h * BLOCK:(h + 1) * BLOCK, :] = jnp.where(lane // IDX_DIM == h % per, c, 0.0).astype(BF16)
        wb_ref[h] = jnp.broadcast_to(wi[:, h:h + 1], (BLOCK, LANES))
    qb = qb_ref[0].astype(F32)
    for j in range(4):
        c = qb[:, j * LANES:(j + 1) * LANES]
        qs_ref[j * BLOCK:(j + 1) * BLOCK, :] = jnp.where(lane < HEAD_DIM, c, 0.0).astype(BF16)
        qs_ref[(4 + j) * BLOCK:(5 + j) * BLOCK, :] = jnp.where(lane < HEAD_DIM, 0.0, c).astype(BF16)

    def idx_body(c, carry):
        kc = ki_ref[0, chunk_rows(c), :]
        sc = lax.dot_general(qim_ref[...], kc, NT_DIMS, preferred_element_type=F32)
        acc = jnp.zeros((BLOCK, CK), F32)
        for h in range(IDX_HEADS):
            w = wb_ref[h]
            acc = acc + jnp.maximum(sc[h * BLOCK:(h + 1) * BLOCK], 0.0) * jnp.concatenate([w, w], 1)
        acc = jnp.where(acc == 0.0, 0.0, acc)
        idx = jnp.where(causal(c), acc, NEG)
        bits = lax.bitcast_convert_type(idx, jnp.int32)
        key_ref[c] = bits ^ ((bits >> 31) & INT_MAX)
        return carry

    lax.fori_loop(0, nck, idx_body, 0)

    def count(pred):
        def body(c, acc):
            m = pred(key_ref[c], c)
            return acc + m[:, :LANES].astype(jnp.int32) + m[:, LANES:].astype(jnp.int32)
        acc = lax.fori_loop(0, nck, body, jnp.zeros((BLOCK, LANES), jnp.int32))
        return jnp.sum(acc, axis=1, keepdims=True)

    def count_ge(cand):
        cb = jnp.broadcast_to(cand, (BLOCK, CK))
        return count(lambda k, c: k >= cb)

    t0 = jnp.full((BLOCK, 1), INT_MIN, jnp.int32)
    n0 = jnp.zeros((BLOCK, 1), jnp.int32) + nck * CK
    cand = jnp.zeros((BLOCK, 1), jnp.int32)
    cnt = count_ge(cand)
    ok = cnt >= TOPK_MAX
    t0, n0 = jnp.where(ok, cand, t0), jnp.where(ok, cnt, n0)

    def bit_body(i, carry):
        t, nge = carry
        cand = t | (jnp.int32(1) << (30 - i))
        cnt = count_ge(cand)
        ok = cnt >= TOPK_MAX
        return jnp.where(ok, cand, t), jnp.where(ok, cnt, nge)

    thr, nge = lax.fori_loop(0, 31, bit_body, (t0, n0))

    def tie_cut():
        tb = jnp.broadcast_to(thr, (BLOCK, CK))
        need = TOPK_MAX - count(lambda k, c: k > tb)

        def jbit(i, jcut):
            cj = jnp.broadcast_to(jcut | (jnp.int32(1) << (11 - i)), (BLOCK, CK))
            below = count(lambda k, c: jnp.where(k == tb, key_pos(c), INT_MAX) < cj)
            return jnp.where(below < need, jcut | (jnp.int32(1) << (11 - i)), jcut)

        jcut = lax.fori_loop(0, 12, jbit, jnp.zeros((BLOCK, 1), jnp.int32))
        return jnp.where(nge > TOPK_MAX, jcut, INT_MAX)

    jcut = lax.cond(jnp.max(nge) > TOPK_MAX, tie_cut,
                    lambda: jnp.full((BLOCK, 1), INT_MAX, jnp.int32))

    def sel_body(c, carry):
        k = key_ref[c]
        keep = (k > thr) | ((k == thr) & (key_pos(c) <= jcut))
        sel_ref[c] = (keep & causal(c)).astype(F32)
        return carry

    lax.fori_loop(0, nck, sel_body, 0)

    macc_ref[...] = jnp.full(macc_ref.shape, -jnp.inf, F32)
    lacc_ref[...] = jnp.zeros(lacc_ref.shape, F32)
    oacc_ref[...] = jnp.zeros(oacc_ref.shape, F32)

    def qk_body(c, carry):
        kc = kb_ref[0, chunk_rows(c), :]
        s = lax.dot_general(qs_ref[...], kc, NT_DIMS, preferred_element_type=F32)
        sel = sel_ref[c] > 0.0
        for h in range(DSA_Q_HEADS):
            rows = slice(h * BLOCK, (h + 1) * BLOCK)
            sh = jnp.where(sel, s[rows], NEG)
            s_ref[c, rows, :] = sh
            macc_ref[rows, :] = jnp.maximum(macc_ref[rows, :], jnp.maximum(sh[:, :LANES], sh[:, LANES:]))
        return carry

    lax.fori_loop(0, nck, qk_body, 0)
    m = jnp.max(macc_ref[...], axis=1, keepdims=True)
    mb_ref[...] = jnp.broadcast_to(m, mb_ref.shape)

    def pv_body(c, carry):
        mb = mb_ref[...]
        p = jnp.exp(s_ref[c] - jnp.concatenate([mb, mb], 1))
        lacc_ref[...] += p[:, :LANES] + p[:, LANES:]
        pb = p.astype(BF16)
        half = 4 * BLOCK
        oacc_ref[...] += (jnp.dot(pb[:half], vlo_ref[0, chunk_rows(c), :], preferred_element_type=F32)
                          + jnp.dot(pb[half:], vhi_ref[0, chunk_rows(c), :], preferred_element_type=F32))
        return carry

    lax.fori_loop(0, nck, pv_body, 0)
    r = 1.0 / jnp.sum(lacc_ref[...], axis=1, keepdims=True)
    lane4 = lax.broadcasted_iota(jnp.int32, (4 * BLOCK, LANES), 1)
    o = oacc_ref[...] * jnp.where(lane4 < HEAD_DIM, r[:4 * BLOCK], r[4 * BLOCK:])
    for j in range(4):
        o_ref[0, :, j * LANES:(j + 1) * LANES] = o[j * BLOCK:(j + 1) * BLOCK].astype(o_ref.dtype)


def _dsa(qi, wi, qb, ki, kb, vlo, vhi):
    b, s, _ = qb.shape
    nck_max = s // CK
    cur = lambda w: pl.BlockSpec((1, BLOCK, w), lambda bi, n: (bi, n, 0))
    seq = pl.BlockSpec((1, s, LANES), lambda bi, n: (bi, 0, 0))
    rows = DSA_Q_HEADS * BLOCK
    return pl.pallas_call(
        _dsa_kernel,
        grid=(b, s // BLOCK),
        in_specs=[cur(QI_W), cur(LANES), cur(QB_W), seq, seq, seq, seq],
        out_specs=cur(QB_W),
        out_shape=jax.ShapeDtypeStruct((b, s, QB_W), BF16),
        scratch_shapes=[
            pltpu.VMEM((IDX_HEADS * BLOCK, LANES), BF16),
            pltpu.VMEM((IDX_HEADS, BLOCK, LANES), F32),
            pltpu.VMEM((rows, LANES), BF16),
            pltpu.VMEM((nck_max, BLOCK, CK), jnp.int32),
            pltpu.VMEM((nck_max, BLOCK, CK), F32),
            pltpu.VMEM((nck_max, rows, CK), F32),
            pltpu.VMEM((rows, LANES), F32),
            pltpu.VMEM((rows, LANES), F32),
            pltpu.VMEM((rows, LANES), F32),
            pltpu.VMEM((4 * BLOCK, LANES), F32),
        ],
        compiler_params=pltpu.CompilerParams(vmem_limit_bytes=VMEM_LIMIT),
        name="dsa_attn",
    )(qi, wi, qb, ki, kb, vlo, vhi)


def _out_proj_kernel(alpha, oa_ref, ob_ref, x_ref, wo_ref, g_ref, b_ref, x1_ref, x1b_ref):
    mix = (jnp.dot(oa_ref[...], wo_ref[:QA_W, :], preferred_element_type=F32)
           + jnp.dot(ob_ref[...], wo_ref[QA_W:, :], preferred_element_type=F32))
    y = _ln(alpha * x_ref[...] + mix, g_ref[...], b_ref[...])
    x1_ref[...] = y
    x1b_ref[...] = y.astype(BF16)


def _out_proj(oa, ob, x, wo, g, b, alpha):
    t = x.shape[0]
    tm = 512
    row = lambda w: pl.BlockSpec((tm, w), lambda i: (i, 0))
    const = lambda shape: pl.BlockSpec(shape, lambda i: (0, 0))
    return pl.pallas_call(
        functools.partial(_out_proj_kernel, alpha),
        grid=(t // tm,),
        in_specs=[row(QA_W), row(QB_W), row(D_MODEL), const(wo.shape),
                  const((1, D_MODEL)), const((1, D_MODEL))],
        out_specs=[row(D_MODEL), row(D_MODEL)],
        out_shape=[jax.ShapeDtypeStruct((t, D_MODEL), F32), jax.ShapeDtypeStruct((t, D_MODEL), BF16)],
        compiler_params=pltpu.CompilerParams(vmem_limit_bytes=VMEM_LIMIT),
        name="out_proj_ln",
    )(oa, ob, x, wo, g, b)


def _ffn_kernel(alpha, d_ff, x1_ref, x1b_ref, p_ref, wgu_ref, wd_ref, wpg_ref, wpp_ref, g_ref, b_ref,
                y_ref, yb_ref, h_ref):
    xb = x1b_ref[...]
    fc = 2 * LANES
    for j in range(d_ff // fc):
        g = jnp.dot(xb, wgu_ref[:, j * fc:(j + 1) * fc], preferred_element_type=F32)
        u = jnp.dot(xb, wgu_ref[:, d_ff + j * fc:d_ff + (j + 1) * fc], preferred_element_type=F32)
        h_ref[:, j * fc:(j + 1) * fc] = (g * jax.nn.sigmoid(g) * u).astype(BF16)
    ffn = jnp.dot(h_ref[...], wd_ref[...], preferred_element_type=F32)
    gate = jax.nn.sigmoid(jnp.dot(xb, wpg_ref[...], preferred_element_type=F32))
    ple = jnp.dot(p_ref[...].astype(BF16), wpp_ref[...], preferred_element_type=F32) * gate
    y = _ln(alpha * x1_ref[...] + ffn + ple, g_ref[...], b_ref[...])
    y_ref[...] = y
    yb_ref[...] = y.astype(BF16)


def _ffn(x1, x1b, p, wgu, wd, wpg, wpp, g, b, alpha):
    t = x1.shape[0]
    d_ff = wd.shape[0]
    tm = 512
    row = lambda w: pl.BlockSpec((tm, w), lambda i: (i, 0))
    const = lambda shape: pl.BlockSpec(shape, lambda i: (0, 0), pipeline_mode=pl.Buffered(1))
    return pl.pallas_call(
        functools.partial(_ffn_kernel, alpha, d_ff),
        grid=(t // tm,),
        in_specs=[row(D_MODEL), row(D_MODEL), row(PLE_DIM), const(wgu.shape), const(wd.shape),
                  const(wpg.shape), const(wpp.shape), const((1, D_MODEL)), const((1, D_MODEL))],
        out_specs=[row(D_MODEL), row(D_MODEL)],
        out_shape=[jax.ShapeDtypeStruct((t, D_MODEL), F32), jax.ShapeDtypeStruct((t, D_MODEL), BF16)],
        scratch_shapes=[pltpu.VMEM((tm, d_ff), BF16)],
        compiler_params=pltpu.CompilerParams(vmem_limit_bytes=VMEM_LIMIT),
        name="ffn_ple_ln",
    )(x1, x1b, p, wgu, wd, wpg, wpp, g, b)


def _extend_w_in(w):
    cuts = [0]
    for sz in IN_SIZES:
        cuts.append(cuts[-1] + sz)
    qa, ka, va, qb, kb, vb, qi, ki, wi = [w[:, cuts[i]:cuts[i + 1]] for i in range(len(IN_SIZES))]
    z64 = jnp.zeros_like(vb)
    pad = jnp.zeros((w.shape[0], LANES - IDX_HEADS), w.dtype)
    ext = jnp.concatenate([qa, ka, va, qb, kb, kb, vb, z64, z64, vb, qi, ki, ki, ki, ki, wi, pad], axis=1)
    return ext.astype(BF16)


def kernel(x, p, positions, ln_in_g, ln_in_b, w_in, attn_sinks, idx_k_g, idx_k_b, w_o, ln1_g, ln1_b,
           w_gu, w_down, w_pg, w_pp, ln2_g, ln2_b):
    bsz, seq, d = x.shape
    depth = w_in.shape[0]
    t = bsz * seq
    alpha = (2 * depth) ** 0.25
    row = lambda v: v.reshape(1, -1).astype(F32)

    inv64 = ROPE_THETA ** (-jnp.arange(HEAD_DIM // 2, dtype=F32) / (HEAD_DIM // 2))
    inv32 = ROPE_THETA ** (-jnp.arange(IDX_DIM // 2, dtype=F32) / (IDX_DIM // 2))
    inv_row = jnp.concatenate([inv64, inv32, jnp.zeros((LANES - 48,), F32)]).reshape(1, LANES)
    tables = _rope_tables(positions.reshape(t, 1), inv_row)

    xf = x.reshape(t, d)
    xb = None
    for i in range(depth):
        w_ext = _extend_w_in(w_in[i])
        kig = jnp.tile(idx_k_g[i], LANES // IDX_DIM).reshape(1, LANES)
        kib = jnp.tile(idx_k_b[i], LANES // IDX_DIM).reshape(1, LANES)
        if i == 0:
            xf, *segs = _in_proj(xf, w_ext, tables, kig, kib, ln=(row(ln_in_g), row(ln_in_b)))
        else:
            segs = _in_proj(xb, w_ext, tables, kig, kib)
        qa, ka, va, qb, kb, vlo, vhi, qi, ki, wi = [a.reshape(bsz, seq, -1) for a in segs]
        sinks = attn_sinks[i].astype(F32)
        snk1 = jnp.repeat(sinks[jnp.array([0, 2, 5, 7])], BLOCK).reshape(-1, 1)
        snk2 = jnp.repeat(sinks[jnp.array([1, 3, 4, 6])], BLOCK).reshape(-1, 1)
        oa = _swa(qa, ka, va, snk1, snk2).reshape(t, QA_W)
        ob = _dsa(qi, wi, qb, ki, kb, vlo, vhi).reshape(t, QB_W)
        x1, x1b = _out_proj(oa, ob, xf, w_o[i].astype(BF16), row(ln1_g[i]), row(ln1_b[i]), alpha)
        xf, xb = _ffn(x1, x1b, p[i].reshape(t, PLE_DIM), w_gu[i].astype(BF16), w_down[i].astype(BF16),
                      w_pg[i].astype(BF16), w_pp[i].astype(BF16), row(ln2_g[i]), row(ln2_b[i]), alpha)
    return xf.reshape(bsz, seq, d)
```

```python
import functools

import jax
import jax.numpy as jnp
from jax import lax
from jax.experimental import pallas as pl
from jax.experimental.pallas import tpu as pltpu

D_MODEL = 1024
HEAD_DIM = 64
SWA_Q_HEADS = 8
SWA_KV_HEADS = 2
DSA_Q_HEADS = 8
IDX_HEADS = 8
IDX_DIM = 32
TOPK_MAX = 256
WINDOW = 128
BLOCK = 128
ROPE_THETA = 10000.0
PLE_DIM = 256
LN_EPS = 1e-5
NEG = -1e30
LANES = 128
QA_W = SWA_Q_HEADS * HEAD_DIM
QB_W = DSA_Q_HEADS * HEAD_DIM
QI_W = IDX_HEADS * IDX_DIM
IN_SIZES = (QA_W, SWA_KV_HEADS * HEAD_DIM, SWA_KV_HEADS * HEAD_DIM, QB_W, HEAD_DIM, HEAD_DIM,
            QI_W, IDX_DIM, IDX_HEADS)
EXT_SEGS = (("qa", QA_W), ("ka", LANES), ("va", LANES), ("qb", QB_W), ("kb", LANES),
            ("vt", LANES), ("qi", QI_W), ("ki", LANES), ("wi", LANES))
EXT_W = sum(w for _, w in EXT_SEGS)
SUB = 256
UNIT = 2 * SUB
PAT_ONE = 0x00800000
INT_MIN = -2 ** 31
INT_MAX = 2 ** 31 - 1
VMEM_LIMIT = 56 * 1024 * 1024

F32 = jnp.float32
BF16 = jnp.bfloat16
NT_DIMS = (((1,), (1,)), ((), ()))


def _ln(x, g, b):
    mu = jnp.mean(x, axis=-1, keepdims=True)
    xc = x - mu
    var = jnp.mean(xc * xc, axis=-1, keepdims=True)
    return xc * lax.rsqrt(var + LN_EPS) * g + b


def _rope_table_kernel(pos_ref, inv_ref, cos64_ref, sin64_ref, cos32_ref, sin32_ref):
    ang = pos_ref[...].astype(F32) * inv_ref[...]
    lane = lax.broadcasted_iota(jnp.int32, ang.shape, 1)
    c, s = jnp.cos(ang), jnp.sin(ang)

    def tile64(a):
        a = jnp.where(lane < 32, a, pltpu.roll(a, 32, 1))
        return jnp.where(lane < 64, a, pltpu.roll(a, 64, 1))

    def tile32(a):
        a = pltpu.roll(a, 96, 1)
        a = jnp.where(lane < 16, a, pltpu.roll(a, 16, 1))
        a = jnp.where(lane < 32, a, pltpu.roll(a, 32, 1))
        return jnp.where(lane < 64, a, pltpu.roll(a, 64, 1))

    cos64_ref[...] = tile64(c)
    sin64_ref[...] = jnp.where(lane % 64 < 32, -1.0, 1.0) * tile64(s)
    cos32_ref[...] = tile32(c)
    sin32_ref[...] = jnp.where(lane % 32 < 16, -1.0, 1.0) * tile32(s)


def _rope_tables(pos_col, inv_row):
    t = pos_col.shape[0]
    tm = 1024
    row = pl.BlockSpec((tm, LANES), lambda i: (i, 0))
    return pl.pallas_call(
        _rope_table_kernel,
        grid=(t // tm,),
        in_specs=[pl.BlockSpec((tm, 1), lambda i: (i, 0)),
                  pl.BlockSpec((1, LANES), lambda i: (0, 0))],
        out_specs=[row] * 4,
        out_shape=[jax.ShapeDtypeStruct((t, LANES), F32)] * 4,
        name="rope_tables",
    )(pos_col, inv_row)


def _rope_apply(h, cos, sin_signed, half):
    lane = lax.broadcasted_iota(jnp.int32, h.shape, 1)
    partner = jnp.where(lane % (2 * half) < half,
                        pltpu.roll(h, LANES - half, 1), pltpu.roll(h, half, 1))
    return h * cos + partner * sin_signed


def _in_proj_kernel(apply_ln, *refs):
    if apply_ln:
        (x_ref, lng_ref, lnb_ref, w_ref, cos64_ref, sin64_ref, cos32_ref, sin32_ref,
         kig_ref, kib_ref, xn_ref, *outs) = refs
        xn = _ln(x_ref[...], lng_ref[...], lnb_ref[...])
        xn_ref[...] = xn
        xb = xn.astype(BF16)
    else:
        (x_ref, w_ref, cos64_ref, sin64_ref, cos32_ref, sin32_ref,
         kig_ref, kib_ref, *outs) = refs
        xb = x_ref[...]
    out = dict(zip([n for n, _ in EXT_SEGS], outs))
    cos64, sin64 = cos64_ref[...], sin64_ref[...]
    cos32, sin32 = cos32_ref[...], sin32_ref[...]
    chunks = [(name, j) for name, width in EXT_SEGS for j in range(width // LANES)]
    mxu_n = 2 * LANES
    for c0 in range(0, len(chunks), mxu_n // LANES):
        group = chunks[c0:c0 + mxu_n // LANES]
        hh = jnp.dot(xb, w_ref[:, c0 * LANES:(c0 + len(group)) * LANES], preferred_element_type=F32)
        for k, (name, j) in enumerate(group):
            h = hh[:, k * LANES:(k + 1) * LANES]
            if name in ("qa", "qb"):
                h = _rope_apply(h, cos64, sin64, HEAD_DIM // 2) * (HEAD_DIM ** -0.5)
            elif name in ("ka", "kb"):
                h = _rope_apply(h, cos64, sin64, HEAD_DIM // 2)
            elif name == "qi":
                h = _rope_apply(h, cos32, sin32, IDX_DIM // 2)
            elif name == "ki":
                h = _rope_apply(_ln(h, kig_ref[...], kib_ref[...]), cos32, sin32, IDX_DIM // 2)
            elif name == "wi":
                h = h * (IDX_HEADS ** -0.5 * IDX_DIM ** -0.5)
            if name == "vt":
                for r in range(h.shape[0] // SUB):
                    out[name][r] = h[r * SUB:(r + 1) * SUB].T.astype(BF16)
            else:
                out[name][:, j * LANES:(j + 1) * LANES] = h.astype(out[name].dtype)


def _in_proj(x, w_ext, tables, kig, kib, ln=None):
    t = x.shape[0]
    tm = 512
    row = lambda w: pl.BlockSpec((tm, w), lambda i: (i, 0))
    const = lambda shape: pl.BlockSpec(shape, lambda i: (0, 0))
    in_specs = [row(D_MODEL)]
    args = [x]
    if ln is not None:
        in_specs += [const((1, D_MODEL))] * 2
        args += list(ln)
    in_specs += [const((D_MODEL, EXT_W))] + [row(LANES)] * 4 + [const((1, LANES))] * 2
    args += [w_ext, *tables, kig, kib]
    out_specs, out_shape = [], []
    if ln is not None:
        out_specs.append(row(D_MODEL))
        out_shape.append(jax.ShapeDtypeStruct((t, D_MODEL), F32))
    for name, width in EXT_SEGS:
        if name == "vt":
            out_specs.append(pl.BlockSpec((tm // SUB, LANES, SUB), lambda i: (i, 0, 0)))
            out_shape.append(jax.ShapeDtypeStruct((t // SUB, LANES, SUB), BF16))
            continue
        out_specs.append(row(width))
        out_shape.append(jax.ShapeDtypeStruct((t, width), F32 if name == "wi" else BF16))
    return pl.pallas_call(
        functools.partial(_in_proj_kernel, ln is not None),
        grid=(t // tm,),
        in_specs=in_specs, out_specs=out_specs, out_shape=out_shape,
        compiler_params=pltpu.CompilerParams(vmem_limit_bytes=VMEM_LIMIT),
        name="in_proj",
    )(*args)


def _swa_kernel(q_ref, kc_ref, kp_ref, vc_ref, vp_ref, snk1_ref, snk2_ref, o_ref):
    n = pl.program_id(1)
    lane = lax.broadcasted_iota(jnp.int32, (BLOCK, LANES), 1)
    lo = lane < HEAD_DIM
    q = q_ref[0].astype(F32)
    qc = [q[:, j * LANES:(j + 1) * LANES] for j in range(4)]
    keep_lo = lambda a: jnp.where(lo, a, 0.0)
    keep_hi = lambda a: jnp.where(lo, 0.0, a)
    lhs1 = jnp.concatenate([keep_lo(qc[0]), keep_lo(qc[1]), keep_hi(qc[2]), keep_hi(qc[3])], 0)
    lhs2 = jnp.concatenate([keep_hi(qc[0]), keep_hi(qc[1]), keep_lo(qc[2]), keep_lo(qc[3])], 0)
    kk = jnp.concatenate([kp_ref[0], kc_ref[0]], 0).astype(F32)
    kr = pltpu.roll(kk, HEAD_DIM, 1)
    s1 = lax.dot_general(lhs1.astype(BF16), kk.astype(BF16), NT_DIMS, preferred_element_type=F32)
    s2 = lax.dot_general(lhs2.astype(BF16), kr.astype(BF16), NT_DIMS, preferred_element_type=F32)
    i = lax.broadcasted_iota(jnp.int32, (4 * BLOCK, 2 * BLOCK), 0) % BLOCK
    j = lax.broadcasted_iota(jnp.int32, (4 * BLOCK, 2 * BLOCK), 1)
    diff = BLOCK + i - j
    first_key = jnp.where(n > 0, 0, BLOCK)
    mask = (diff >= 0) & (diff < WINDOW) & (j >= first_key)

    def softmax(s, sink):
        s = jnp.where(mask, s, NEG)
        m = jnp.maximum(jnp.max(s, axis=-1, keepdims=True), sink)
        p = jnp.exp(s - m)
        denom = jnp.sum(p, axis=-1, keepdims=True) + jnp.exp(sink - m)
        return p.astype(BF16), 1.0 / denom

    p1, r1 = softmax(s1, snk1_ref[...])
    p2, r2 = softmax(s2, snk2_ref[...])
    vv = jnp.concatenate([vp_ref[0], vc_ref[0]], 0).astype(F32)
    vr = pltpu.roll(vv, HEAD_DIM, 1)
    lo2 = jnp.concatenate([lo, lo], 0)
    v_lo = lambda a: jnp.where(lo2, a, 0.0).astype(BF16)
    v_hi = lambda a: jnp.where(lo2, 0.0, a).astype(BF16)
    h = 2 * BLOCK
    dot = functools.partial(jnp.dot, preferred_element_type=F32)
    o01 = dot(p1[:h], v_lo(vv)) + dot(p2[:h], v_hi(vr))
    o23 = dot(p2[h:], v_lo(vr)) + dot(p1[h:], v_hi(vv))
    o01 = o01 * jnp.where(lo2, r1[:h], r2[:h])
    o23 = o23 * jnp.where(lo2, r2[h:], r1[h:])
    o_ref[0, :, 0 * LANES:1 * LANES] = o01[:BLOCK].astype(o_ref.dtype)
    o_ref[0, :, 1 * LANES:2 * LANES] = o01[BLOCK:].astype(o_ref.dtype)
    o_ref[0, :, 2 * LANES:3 * LANES] = o23[:BLOCK].astype(o_ref.dtype)
    o_ref[0, :, 3 * LANES:4 * LANES] = o23[BLOCK:].astype(o_ref.dtype)


def _swa(qa, ka, va, snk1, snk2):
    b, s, _ = qa.shape
    cur = lambda w: pl.BlockSpec((1, BLOCK, w), lambda bi, n: (bi, n, 0))
    prev = lambda w: pl.BlockSpec((1, BLOCK, w), lambda bi, n: (bi, jnp.maximum(n - 1, 0), 0))
    snk = pl.BlockSpec((4 * BLOCK, 1), lambda bi, n: (0, 0))
    return pl.pallas_call(
        _swa_kernel,
        grid=(b, s // BLOCK),
        in_specs=[cur(QA_W), cur(LANES), prev(LANES), cur(LANES), prev(LANES), snk, snk],
        out_specs=cur(QA_W),
        out_shape=jax.ShapeDtypeStruct((b, s, QA_W), BF16),
        name="swa_attn",
    )(qa, ka, ka, va, va, snk1, snk2)


def _pattern(digit):
    return lax.bitcast_convert_type((digit << 16) + PAT_ONE, F32)


def _tree_sum(parts):
    while len(parts) > 1:
        parts = [a + b for a, b in zip(parts[::2], parts[1::2])]
    return parts[0]


def _dsa_kernel(qi_ref, wi_ref, qb_ref, ki_ref, kb_ref, vt_ref, o_ref,
                qim_ref, qs_ref, key_ref, atop_ref, amid_ref, alow_ref, a2_ref, a3_ref,
                sel_ref, s_ref, oacc_ref):
    n = pl.program_id(1)
    nu = n // (UNIT // BLOCK) + 1
    lane = lax.broadcasted_iota(jnp.int32, (BLOCK, LANES), 1)
    sub_rows = lambda u, j: pl.ds(pl.multiple_of(u * UNIT + j * SUB, SUB), SUB)
    unit_rows = lambda u: pl.ds(pl.multiple_of(u * UNIT, UNIT), UNIT)
    key_pos = lambda u, j: u * UNIT + j * SUB + lax.broadcasted_iota(jnp.int32, (SUB, LANES), 0)
    causal = lambda u, j: key_pos(u, j) <= n * BLOCK + lax.broadcasted_iota(jnp.int32, (SUB, LANES), 1)
    head = lambda a, h: a[:, h * LANES:(h + 1) * LANES]

    qi = qi_ref[0].astype(F32)
    for h in range(IDX_HEADS):
        per = LANES // IDX_DIM
        qim_ref[h * BLOCK:(h + 1) * BLOCK, :] = jnp.where(
            lane // IDX_DIM == h % per, head(qi, h // per), 0.0).astype(BF16)
    qb = qb_ref[0].astype(F32)
    for h in range(DSA_Q_HEADS):
        keep = (lane < HEAD_DIM) if h % 2 == 0 else (lane >= HEAD_DIM)
        qs_ref[h * BLOCK:(h + 1) * BLOCK, :] = jnp.where(keep, head(qb, h // 2), 0.0).astype(BF16)
    wt = wi_ref[0].T
    w_rows = [wt[h:h + 1, :] for h in range(IDX_HEADS)]

    def idx_unit(u, carry):
        for j in range(UNIT // SUB):
            rows = sub_rows(u, j)
            sc = lax.dot_general(ki_ref[0, rows, :], qim_ref[...], NT_DIMS,
                                 preferred_element_type=F32)
            acc = jnp.maximum(head(sc, 0), 0.0) * w_rows[0]
            for h in range(1, IDX_HEADS):
                acc = acc + jnp.maximum(head(sc, h), 0.0) * w_rows[h]
            acc = jnp.where(acc == 0.0, 0.0, acc)
            idx = jnp.where(causal(u, j), acc, NEG)
            bits = lax.bitcast_convert_type(idx, jnp.int32)
            key = bits ^ ((bits >> 31) & INT_MAX)
            key_ref[rows, :] = key
            ukey = key ^ INT_MIN
            digit_bits = 0x3FFF0000
            top = lax.shift_right_logical(ukey, 12) & 0x000F0000
            atop_ref[rows, :] = lax.bitcast_convert_type(top + PAT_ONE, F32).astype(BF16)
            amid_ref[rows, :] = lax.bitcast_convert_type(((ukey << 2) & digit_bits) + PAT_ONE, F32).astype(BF16)
            alow_ref[rows, :] = lax.bitcast_convert_type(((ukey << 16) & digit_bits) + PAT_ONE, F32).astype(BF16)
        return carry

    lax.fori_loop(0, nu, idx_unit, 0)

    grp = 16
    as_groups = lambda a: a.reshape(a.shape[0] // grp, grp, LANES)
    packed = lambda pat: jnp.broadcast_to(pat, (grp, LANES)).astype(BF16)[None]

    def count_ge(a_ref, pat):
        pb = packed(pat)

        def body(u, acc):
            a = as_groups(a_ref[unit_rows(u), :])
            m = jnp.where(a >= pb, jnp.ones_like(a), jnp.zeros_like(a))
            return acc + _tree_sum([m[i] for i in range(m.shape[0])])

        acc = lax.fori_loop(0, nu, body, jnp.zeros((grp, LANES), BF16))
        return jnp.sum(acc.astype(F32), axis=0, keepdims=True)

    def digit_search(a_ref, nbits, need):
        def bit(i, carry):
            d, above = carry
            cand = d | (jnp.int32(1) << (nbits - 1 - i))
            cnt = count_ge(a_ref, _pattern(cand))
            ok = cnt >= need
            return jnp.where(ok, cand, d), jnp.where(ok, above, cnt)

        return lax.fori_loop(0, nbits, bit, (jnp.zeros((1, LANES), jnp.int32), jnp.zeros((1, LANES), F32)))

    def narrow(match_ref, digit, src_ref, dst_ref):
        pb = packed(_pattern(digit))

        def body(u, carry):
            mt, sr = as_groups(match_ref[unit_rows(u), :]), as_groups(src_ref[unit_rows(u), :])
            dst_ref[unit_rows(u), :] = jnp.where(mt == pb, sr, jnp.zeros_like(sr)).reshape(UNIT, LANES)
            return carry

        lax.fori_loop(0, nu, body, 0)

    def count_keys(pred):
        def body(u, acc):
            for j in range(UNIT // SUB):
                m = pred(key_ref[sub_rows(u, j), :], key_pos(u, j)).astype(F32)
                acc = acc + jnp.sum(m.reshape(SUB // 8, 8, LANES), axis=0)
            return acc

        return jnp.sum(lax.fori_loop(0, nu, body, jnp.zeros((8, LANES), F32)), axis=0, keepdims=True)

    def write_sel(keep):
        def body(u, carry):
            for j in range(UNIT // SUB):
                rows = sub_rows(u, j)
                sel_ref[rows, :] = (keep(key_ref[rows, :], key_pos(u, j)) & causal(u, j)).astype(F32)
            return carry

        lax.fori_loop(0, nu, body, 0)

    @pl.when(n * BLOCK < TOPK_MAX)
    def _():
        write_sel(lambda k, kp: k == k)

    @pl.when(n * BLOCK >= TOPK_MAX)
    def _():
        need1 = jnp.full((1, LANES), TOPK_MAX, F32)
        d1, above1 = digit_search(atop_ref, 4, need1)
        narrow(atop_ref, d1, amid_ref, a2_ref)
        need2 = need1 - above1
        d2, above2 = digit_search(a2_ref, 14, need2)
        narrow(a2_ref, d2, alow_ref, a3_ref)
        d3, _ = digit_search(a3_ref, 14, need2 - above2)
        thr = ((d1 << 28) | (d2 << 14) | d3) ^ INT_MIN
        write_sel(lambda k, kp: k >= thr)
        nge = count_keys(lambda k, kp: k >= thr)

        @pl.when(jnp.max(nge) > TOPK_MAX)
        def _():
            need = TOPK_MAX - count_keys(lambda k, kp: k > thr)

            def jbit(i, jcut):
                cj = jcut | (jnp.int32(1) << (pos_bits - 1 - i))
                below = count_keys(lambda k, kp: (k == thr) & (kp < cj))
                return jnp.where(below < need, cj, jcut)

            pos_bits = (ki_ref.shape[1] - 1).bit_length()
            jcut = lax.fori_loop(0, pos_bits, jbit, jnp.zeros((1, LANES), jnp.int32))
            jcut = jnp.where(nge > TOPK_MAX, jcut, INT_MAX)
            write_sel(lambda k, kp: (k > thr) | ((k == thr) & (kp <= jcut)))

    def qk_unit(u, macc):
        for j in range(UNIT // SUB):
            rows = sub_rows(u, j)
            s = lax.dot_general(kb_ref[0, rows, :], qs_ref[...], NT_DIMS, preferred_element_type=F32)
            sel = sel_ref[rows, :] > 0.0
            mx = []
            for h in range(DSA_Q_HEADS):
                sh = jnp.where(sel, head(s, h), NEG)
                s_ref[rows, h * LANES:(h + 1) * LANES] = sh
                mx.append(jnp.max(sh.reshape(SUB // 8, 8, LANES), axis=0))
            macc = jnp.maximum(macc, jnp.concatenate(mx, axis=1))
        return macc

    width = DSA_Q_HEADS * LANES
    macc = lax.fori_loop(0, nu, qk_unit, jnp.full((8, width), -jnp.inf, F32))
    m = jnp.max(macc, axis=0, keepdims=True)
    oacc_ref[...] = jnp.zeros(oacc_ref.shape, F32)

    def pv_unit(u, lacc):
        for j in range(UNIT // SUB):
            p = jnp.exp(s_ref[sub_rows(u, j), :] - m)
            lacc = lacc + jnp.sum(p.reshape(SUB // 8, 8, width), axis=0)
            oacc_ref[...] += jnp.dot(vt_ref[u * (UNIT // SUB) + j], p.astype(BF16),
                                     preferred_element_type=F32)
        return lacc

    lacc = lax.fori_loop(0, nu, pv_unit, jnp.zeros((8, width), F32))
    r = 1.0 / jnp.sum(lacc, axis=0, keepdims=True)
    o = oacc_ref[...]
    row = lax.broadcasted_iota(jnp.int32, (LANES, BLOCK), 0)
    for j in range(DSA_Q_HEADS // 2):
        pair = jnp.where(row < HEAD_DIM, head(o, 2 * j) * head(r, 2 * j),
                         head(o, 2 * j + 1) * head(r, 2 * j + 1))
        o_ref[0, :, j * LANES:(j + 1) * LANES] = pair.T.astype(o_ref.dtype)


def _dsa(qi, wi, qb, ki, kb, vt):
    b, s, _ = qb.shape
    cur = lambda w: pl.BlockSpec((1, BLOCK, w), lambda bi, n: (bi, n, 0))
    seq = pl.BlockSpec((1, s, LANES), lambda bi, n: (bi, 0, 0))
    rows = DSA_Q_HEADS * BLOCK
    digits = pltpu.VMEM((s, LANES), BF16)
    return pl.pallas_call(
        _dsa_kernel,
        grid=(b, s // BLOCK),
        in_specs=[cur(QI_W), cur(LANES), cur(QB_W), seq, seq,
                  pl.BlockSpec((s // SUB, LANES, SUB), lambda bi, n: (bi, 0, 0))],
        out_specs=cur(QB_W),
        out_shape=jax.ShapeDtypeStruct((b, s, QB_W), BF16),
        scratch_shapes=[
            pltpu.VMEM((IDX_HEADS * BLOCK, LANES), BF16),
            pltpu.VMEM((rows, LANES), BF16),
            pltpu.VMEM((s, LANES), jnp.int32),
            digits, digits, digits,
            digits, digits,
            pltpu.VMEM((s, LANES), F32),
            pltpu.VMEM((s, rows), F32),
            pltpu.VMEM((LANES, rows), F32),
        ],
        compiler_params=pltpu.CompilerParams(vmem_limit_bytes=VMEM_LIMIT),
        name="dsa_attn",
    )(qi, wi, qb, ki, kb, vt)


def _out_proj_kernel(alpha, oa_ref, ob_ref, x_ref, wo_ref, g_ref, b_ref, x1_ref, x1b_ref):
    mix = (jnp.dot(oa_ref[...], wo_ref[:QA_W, :], preferred_element_type=F32)
           + jnp.dot(ob_ref[...], wo_ref[QA_W:, :], preferred_element_type=F32))
    y = _ln(alpha * x_ref[...] + mix, g_ref[...], b_ref[...])
    x1_ref[...] = y
    x1b_ref[...] = y.astype(BF16)


def _out_proj(oa, ob, x, wo, g, b, alpha):
    t = x.shape[0]
    tm = 512
    row = lambda w: pl.BlockSpec((tm, w), lambda i: (i, 0))
    const = lambda shape: pl.BlockSpec(shape, lambda i: (0, 0))
    return pl.pallas_call(
        functools.partial(_out_proj_kernel, alpha),
        grid=(t // tm,),
        in_specs=[row(QA_W), row(QB_W), row(D_MODEL), const(wo.shape),
                  const((1, D_MODEL)), const((1, D_MODEL))],
        out_specs=[row(D_MODEL), row(D_MODEL)],
        out_shape=[jax.ShapeDtypeStruct((t, D_MODEL), F32), jax.ShapeDtypeStruct((t, D_MODEL), BF16)],
        compiler_params=pltpu.CompilerParams(vmem_limit_bytes=VMEM_LIMIT),
        name="out_proj_ln",
    )(oa, ob, x, wo, g, b)


def _ffn_kernel(alpha, d_ff, x1_ref, x1b_ref, p_ref, wgu_ref, wd_ref, wpg_ref, wpp_ref, g_ref, b_ref,
                y_ref, yb_ref, h_ref):
    xb = x1b_ref[...]
    fc = 2 * LANES
    for j in range(d_ff // fc):
        g = jnp.dot(xb, wgu_ref[:, j * fc:(j + 1) * fc], preferred_element_type=F32)
        u = jnp.dot(xb, wgu_ref[:, d_ff + j * fc:d_ff + (j + 1) * fc], preferred_element_type=F32)
        h_ref[:, j * fc:(j + 1) * fc] = (g * jax.nn.sigmoid(g) * u).astype(BF16)
    ffn = jnp.dot(h_ref[...], wd_ref[...], preferred_element_type=F32)
    gate = jax.nn.sigmoid(jnp.dot(xb, wpg_ref[...], preferred_element_type=F32))
    ple = jnp.dot(p_ref[...].astype(BF16), wpp_ref[...], preferred_element_type=F32) * gate
    y = _ln(alpha * x1_ref[...] + ffn + ple, g_ref[...], b_ref[...])
    y_ref[...] = y
    yb_ref[...] = y.astype(BF16)


def _ffn(x1, x1b, p, wgu, wd, wpg, wpp, g, b, alpha):
    t = x1.shape[0]
    d_ff = wd.shape[0]
    tm = 512
    row = lambda w: pl.BlockSpec((tm, w), lambda i: (i, 0))
    const = lambda shape: pl.BlockSpec(shape, lambda i: (0, 0), pipeline_mode=pl.Buffered(1))
    return pl.pallas_call(
        functools.partial(_ffn_kernel, alpha, d_ff),
        grid=(t // tm,),
        in_specs=[row(D_MODEL), row(D_MODEL), row(PLE_DIM), const(wgu.shape), const(wd.shape),
                  const(wpg.shape), const(wpp.shape), const((1, D_MODEL)), const((1, D_MODEL))],
        out_specs=[row(D_MODEL), row(D_MODEL)],
        out_shape=[jax.ShapeDtypeStruct((t, D_MODEL), F32), jax.ShapeDtypeStruct((t, D_MODEL), BF16)],
        scratch_shapes=[pltpu.VMEM((tm, d_ff), BF16)],
        compiler_params=pltpu.CompilerParams(vmem_limit_bytes=VMEM_LIMIT),
        name="ffn_ple_ln",
    )(x1, x1b, p, wgu, wd, wpg, wpp, g, b)


def _extend_w_in(w):
    cuts = [0]
    for sz in IN_SIZES:
        cuts.append(cuts[-1] + sz)
    qa, ka, va, qb, kb, vb, qi, ki, wi = [w[:, cuts[i]:cuts[i + 1]] for i in range(len(IN_SIZES))]
    pad = jnp.zeros((w.shape[0], LANES - IDX_HEADS), w.dtype)
    ext = jnp.concatenate([qa, ka, va, qb, kb, kb, vb, vb, qi, ki, ki, ki, ki, wi, pad], axis=1)
    return ext.astype(BF16)


def kernel(x, p, positions, ln_in_g, ln_in_b, w_in, attn_sinks, idx_k_g, idx_k_b, w_o, ln1_g, ln1_b,
           w_gu, w_down, w_pg, w_pp, ln2_g, ln2_b):
    bsz, seq, d = x.shape
    depth = w_in.shape[0]
    t = bsz * seq
    alpha = (2 * depth) ** 0.25
    row = lambda v: v.reshape(1, -1).astype(F32)

    inv64 = ROPE_THETA ** (-jnp.arange(HEAD_DIM // 2, dtype=F32) / (HEAD_DIM // 2))
    inv32 = ROPE_THETA ** (-jnp.arange(IDX_DIM // 2, dtype=F32) / (IDX_DIM // 2))
    inv_row = jnp.concatenate([inv64, inv32, jnp.zeros((LANES - 48,), F32)]).reshape(1, LANES)
    tables = _rope_tables(positions.reshape(t, 1), inv_row)

    xf = x.reshape(t, d)
    xb = None
    for i in range(depth):
        w_ext = _extend_w_in(w_in[i])
        kig = jnp.tile(idx_k_g[i], LANES // IDX_DIM).reshape(1, LANES)
        kib = jnp.tile(idx_k_b[i], LANES // IDX_DIM).reshape(1, LANES)
        if i == 0:
            xf, *segs = _in_proj(xf, w_ext, tables, kig, kib, ln=(row(ln_in_g), row(ln_in_b)))
        else:
            segs = _in_proj(xb, w_ext, tables, kig, kib)
        qa, ka, va, qb, kb, vt, qi, ki, wi = [a if a.ndim == 3 else a.reshape(bsz, seq, -1) for a in segs]
        sinks = attn_sinks[i].astype(F32)
        snk1 = jnp.repeat(sinks[jnp.array([0, 2, 5, 7])], BLOCK).reshape(-1, 1)
        snk2 = jnp.repeat(sinks[jnp.array([1, 3, 4, 6])], BLOCK).reshape(-1, 1)
        oa = _swa(qa, ka, va, snk1, snk2).reshape(t, QA_W)
        ob = _dsa(qi, wi, qb, ki, kb, vt).reshape(t, QB_W)
        x1, x1b = _out_proj(oa, ob, xf, w_o[i].astype(BF16), row(ln1_g[i]), row(ln1_b[i]), alpha)
        xf, xb = _ffn(x1, x1b, p[i].reshape(t, PLE_DIM), w_gu[i].astype(BF16), w_down[i].astype(BF16),
                      w_pg[i].astype(BF16), w_pp[i].astype(BF16), row(ln2_g[i]), row(ln2_b[i]), alpha)
    return xf.reshape(bsz, seq, d)
```

```python
import functools

import jax
import jax.numpy as jnp
from jax import lax
from jax.experimental import pallas as pl
from jax.experimental.pallas import tpu as pltpu

D_MODEL = 1024
HEAD_DIM = 64
SWA_Q_HEADS = 8
SWA_KV_HEADS = 2
DSA_Q_HEADS = 8
IDX_HEADS = 8
IDX_DIM = 32
TOPK_MAX = 256
WINDOW = 128
BLOCK = 128
ROPE_THETA = 10000.0
PLE_DIM = 256
LN_EPS = 1e-5
NEG = -1e30
LOG2E = 1.4426950408889634
LANES = 128
QA_W = SWA_Q_HEADS * HEAD_DIM
QB_W = DSA_Q_HEADS * HEAD_DIM
QI_W = IDX_HEADS * IDX_DIM
IN_SIZES = (QA_W, SWA_KV_HEADS * HEAD_DIM, SWA_KV_HEADS * HEAD_DIM, QB_W, HEAD_DIM, HEAD_DIM,
            QI_W, IDX_DIM, IDX_HEADS)
EXT_SEGS = (("qa", QA_W), ("ka", LANES), ("va", LANES), ("qb", QB_W), ("kb", LANES),
            ("vt", LANES), ("qi", QI_W), ("ki", LANES), ("wi", LANES))
EXT_W = sum(w for _, w in EXT_SEGS)
SUB = 256
UNIT = 2 * SUB
PAT_ONE = 0x00800000
INT_MIN = -2 ** 31
INT_MAX = 2 ** 31 - 1
VMEM_LIMIT = 56 * 1024 * 1024

F32 = jnp.float32
BF16 = jnp.bfloat16


def _ln(x, g, b):
    mu = jnp.mean(x, axis=-1, keepdims=True)
    xc = x - mu
    var = jnp.mean(xc * xc, axis=-1, keepdims=True)
    return xc * lax.rsqrt(var + LN_EPS) * g + b


def _rope_table_kernel(pos_ref, inv_ref, cos64_ref, sin64_ref, cos32_ref, sin32_ref):
    ang = pos_ref[...].astype(F32) * inv_ref[...]
    lane = lax.broadcasted_iota(jnp.int32, ang.shape, 1)
    c, s = jnp.cos(ang), jnp.sin(ang)

    def tile64(a):
        a = jnp.where(lane < 32, a, pltpu.roll(a, 32, 1))
        return jnp.where(lane < 64, a, pltpu.roll(a, 64, 1))

    def tile32(a):
        a = pltpu.roll(a, 96, 1)
        a = jnp.where(lane < 16, a, pltpu.roll(a, 16, 1))
        a = jnp.where(lane < 32, a, pltpu.roll(a, 32, 1))
        return jnp.where(lane < 64, a, pltpu.roll(a, 64, 1))

    cos64_ref[...] = tile64(c)
    sin64_ref[...] = jnp.where(lane % 64 < 32, -1.0, 1.0) * tile64(s)
    cos32_ref[...] = tile32(c)
    sin32_ref[...] = jnp.where(lane % 32 < 16, -1.0, 1.0) * tile32(s)


def _rope_tables(pos_col, inv_row):
    t = pos_col.shape[0]
    tm = 1024
    row = pl.BlockSpec((tm, LANES), lambda i: (i, 0))
    return pl.pallas_call(
        _rope_table_kernel,
        grid=(t // tm,),
        in_specs=[pl.BlockSpec((tm, 1), lambda i: (i, 0)),
                  pl.BlockSpec((1, LANES), lambda i: (0, 0))],
        out_specs=[row] * 4,
        out_shape=[jax.ShapeDtypeStruct((t, LANES), F32)] * 4,
        name="rope_tables",
    )(pos_col, inv_row)


def _rope_apply(h, cos, sin_signed, half):
    lane = lax.broadcasted_iota(jnp.int32, h.shape, 1)
    partner = jnp.where(lane % (2 * half) < half,
                        pltpu.roll(h, LANES - half, 1), pltpu.roll(h, half, 1))
    return h * cos + partner * sin_signed


def _in_proj_kernel(apply_ln, *refs):
    if apply_ln:
        (x_ref, lng_ref, lnb_ref, w_ref, cos64_ref, sin64_ref, cos32_ref, sin32_ref,
         kig_ref, kib_ref, xn_ref, *outs) = refs
        xn = _ln(x_ref[...], lng_ref[...], lnb_ref[...])
        xn_ref[...] = xn
        xb = xn.astype(BF16)
    else:
        (x_ref, w_ref, cos64_ref, sin64_ref, cos32_ref, sin32_ref,
         kig_ref, kib_ref, *outs) = refs
        xb = x_ref[...]
    out = dict(zip([n for n, _ in EXT_SEGS], outs))
    cos64, sin64 = cos64_ref[...], sin64_ref[...]
    cos32, sin32 = cos32_ref[...], sin32_ref[...]
    chunks = [(name, j) for name, width in EXT_SEGS for j in range(width // LANES)]
    mxu_n = 2 * LANES
    for c0 in range(0, len(chunks), mxu_n // LANES):
        group = chunks[c0:c0 + mxu_n // LANES]
        hh = jnp.dot(xb, w_ref[:, c0 * LANES:(c0 + len(group)) * LANES], preferred_element_type=F32)
        for k, (name, j) in enumerate(group):
            h = hh[:, k * LANES:(k + 1) * LANES]
            if name in ("qa", "qb"):
                h = _rope_apply(h, cos64, sin64, HEAD_DIM // 2) * (HEAD_DIM ** -0.5 * LOG2E)
            elif name in ("ka", "kb"):
                h = _rope_apply(h, cos64, sin64, HEAD_DIM // 2)
            elif name == "qi":
                h = _rope_apply(h, cos32, sin32, IDX_DIM // 2)
            elif name == "ki":
                h = _rope_apply(_ln(h, kig_ref[...], kib_ref[...]), cos32, sin32, IDX_DIM // 2)
            elif name == "wi":
                h = h * (IDX_HEADS ** -0.5 * IDX_DIM ** -0.5)
            if name == "vt":
                for r in range(h.shape[0] // SUB):
                    out[name][r] = h[r * SUB:(r + 1) * SUB].T.astype(BF16)
            else:
                out[name][:, j * LANES:(j + 1) * LANES] = h.astype(out[name].dtype)


def _in_proj(x, w_ext, tables, kig, kib, ln=None):
    t = x.shape[0]
    tm = 512
    row = lambda w: pl.BlockSpec((tm, w), lambda i: (i, 0))
    const = lambda shape: pl.BlockSpec(shape, lambda i: (0, 0))
    in_specs = [row(D_MODEL)]
    args = [x]
    if ln is not None:
        in_specs += [const((1, D_MODEL))] * 2
        args += list(ln)
    in_specs += [const((D_MODEL, EXT_W))] + [row(LANES)] * 4 + [const((1, LANES))] * 2
    args += [w_ext, *tables, kig, kib]
    out_specs, out_shape = [], []
    if ln is not None:
        out_specs.append(row(D_MODEL))
        out_shape.append(jax.ShapeDtypeStruct((t, D_MODEL), F32))
    for name, width in EXT_SEGS:
        if name == "vt":
            out_specs.append(pl.BlockSpec((tm // SUB, LANES, SUB), lambda i: (i, 0, 0)))
            out_shape.append(jax.ShapeDtypeStruct((t // SUB, LANES, SUB), BF16))
            continue
        out_specs.append(row(width))
        out_shape.append(jax.ShapeDtypeStruct((t, width), F32 if name == "wi" else BF16))
    return pl.pallas_call(
        functools.partial(_in_proj_kernel, ln is not None),
        grid=(t // tm,),
        in_specs=in_specs, out_specs=out_specs, out_shape=out_shape,
        compiler_params=pltpu.CompilerParams(vmem_limit_bytes=VMEM_LIMIT),
        name="in_proj",
    )(*args)


def _swa_kernel(q_ref, kc_ref, kp_ref, vc_ref, vp_ref, sink_ref, o_ref):
    n = pl.program_id(1)
    row = lax.broadcasted_iota(jnp.int32, (LANES, BLOCK), 0)
    head = lambda a, h: a[:, h * LANES:(h + 1) * LANES]
    group = SWA_Q_HEADS // SWA_KV_HEADS
    swap = lambda a: pltpu.roll(a, HEAD_DIM, 0)
    q = q_ref[0].astype(F32)
    cols = []
    for j in range(SWA_Q_HEADS // 2):
        qt = head(q, j).T
        kv = (2 * j) // group
        keep = (row < HEAD_DIM) if kv == 0 else (row >= HEAD_DIM)
        even, odd = (qt, swap(qt)) if kv == 0 else (swap(qt), qt)
        cols += [jnp.where(keep, even, 0.0), jnp.where(keep, odd, 0.0)]
    qt = jnp.concatenate(cols, axis=1).astype(BF16)
    kk = jnp.concatenate([kp_ref[0], kc_ref[0]], 0)
    s = jnp.dot(kk, qt, preferred_element_type=F32)
    kidx = lax.broadcasted_iota(jnp.int32, (2 * BLOCK, BLOCK), 0)
    diff = BLOCK + lax.broadcasted_iota(jnp.int32, (2 * BLOCK, BLOCK), 1) - kidx
    first_key = jnp.where(n > 0, 0, BLOCK)
    mask = (diff >= 0) & (diff < WINDOW) & (kidx >= first_key)
    s = jnp.concatenate([jnp.where(mask, head(s, h), NEG) for h in range(SWA_Q_HEADS)], axis=1)
    sink = sink_ref[...] * LOG2E
    m = jnp.maximum(jnp.max(s, axis=0, keepdims=True), sink)
    p = jnp.exp2(s - m)
    r = 1.0 / (jnp.sum(p, axis=0, keepdims=True) + jnp.exp2(sink - m))
    vt = jnp.concatenate([vp_ref[0], vc_ref[0]], 0).astype(F32).T.astype(BF16)
    ot = jnp.dot(vt, p.astype(BF16), preferred_element_type=F32) * r
    for j in range(SWA_Q_HEADS // 2):
        a, b = head(ot, 2 * j), head(ot, 2 * j + 1)
        pair = jnp.where(row < HEAD_DIM, a, swap(b)) if (2 * j) // group == 0 else \
            jnp.where(row < HEAD_DIM, swap(a), b)
        o_ref[0, :, j * LANES:(j + 1) * LANES] = pair.T.astype(o_ref.dtype)


def _swa(qa, ka, va, sink_row):
    b, s, _ = qa.shape
    cur = lambda w: pl.BlockSpec((1, BLOCK, w), lambda bi, n: (bi, n, 0))
    prev = lambda w: pl.BlockSpec((1, BLOCK, w), lambda bi, n: (bi, jnp.maximum(n - 1, 0), 0))
    return pl.pallas_call(
        _swa_kernel,
        grid=(b, s // BLOCK),
        in_specs=[cur(QA_W), cur(LANES), prev(LANES), cur(LANES), prev(LANES),
                  pl.BlockSpec((1, SWA_Q_HEADS * LANES), lambda bi, n: (0, 0))],
        out_specs=cur(QA_W),
        out_shape=jax.ShapeDtypeStruct((b, s, QA_W), BF16),
        name="swa_attn",
    )(qa, ka, ka, va, va, sink_row)


def _pattern(digit):
    return lax.bitcast_convert_type((digit << 16) + PAT_ONE, F32)


def _tree_sum(parts):
    while len(parts) > 1:
        parts = [a + b for a, b in zip(parts[::2], parts[1::2])]
    return parts[0]


def _dsa_kernel(qi_ref, wi_ref, qb_ref, ki_ref, kb_ref, vt_ref, o_ref,
                qim_ref, qs_ref, key_ref, atop_ref, amid_ref, alow_ref, a2_ref, a3_ref,
                sel_ref, s_ref, oacc_ref):
    n = pl.program_id(1)
    nu = n // (UNIT // BLOCK) + 1
    lane = lax.broadcasted_iota(jnp.int32, (BLOCK, LANES), 1)
    sub_rows = lambda u, j: pl.ds(pl.multiple_of(u * UNIT + j * SUB, SUB), SUB)
    unit_rows = lambda u: pl.ds(pl.multiple_of(u * UNIT, UNIT), UNIT)
    key_pos = lambda u, j: u * UNIT + j * SUB + lax.broadcasted_iota(jnp.int32, (SUB, LANES), 0)
    causal = lambda u, j: key_pos(u, j) <= n * BLOCK + lax.broadcasted_iota(jnp.int32, (SUB, LANES), 1)
    head = lambda a, h: a[:, h * LANES:(h + 1) * LANES]

    qi = qi_ref[0].astype(F32)
    for h in range(IDX_HEADS):
        per = LANES // IDX_DIM
        qim_ref[:, h * BLOCK:(h + 1) * BLOCK] = jnp.where(
            lane // IDX_DIM == h % per, head(qi, h // per), 0.0).T.astype(BF16)
    qb = qb_ref[0].astype(F32)
    for h in range(DSA_Q_HEADS):
        keep = (lane < HEAD_DIM) if h % 2 == 0 else (lane >= HEAD_DIM)
        qs_ref[:, h * BLOCK:(h + 1) * BLOCK] = jnp.where(keep, head(qb, h // 2), 0.0).T.astype(BF16)
    wt = wi_ref[0].T
    w_rows = [wt[h:h + 1, :] for h in range(IDX_HEADS)]

    def idx_unit(u, carry):
        for j in range(UNIT // SUB):
            rows = sub_rows(u, j)
            s_ref[rows, :] = jnp.dot(kb_ref[0, rows, :], qs_ref[...], preferred_element_type=F32)
            sc = jnp.dot(ki_ref[0, rows, :], qim_ref[...], preferred_element_type=F32)
            acc = jnp.maximum(head(sc, 0), 0.0) * w_rows[0]
            for h in range(1, IDX_HEADS):
                acc = acc + jnp.maximum(head(sc, h), 0.0) * w_rows[h]
            acc = jnp.where(acc == 0.0, 0.0, acc)
            idx = jnp.where(causal(u, j), acc, NEG)
            bits = lax.bitcast_convert_type(idx, jnp.int32)
            key = bits ^ ((bits >> 31) & INT_MAX)
            key_ref[rows, :] = key
            ukey = key ^ INT_MIN
            digit_bits = 0x3FFF0000
            top = lax.shift_right_logical(ukey, 12) & 0x000F0000
            atop_ref[rows, :] = lax.bitcast_convert_type(top + PAT_ONE, F32).astype(BF16)
            amid_ref[rows, :] = lax.bitcast_convert_type(((ukey << 2) & digit_bits) + PAT_ONE, F32).astype(BF16)
            alow_ref[rows, :] = lax.bitcast_convert_type(((ukey << 16) & digit_bits) + PAT_ONE, F32).astype(BF16)
        return carry

    lax.fori_loop(0, nu, idx_unit, 0)

    grp = 16
    as_groups = lambda a: a.reshape(a.shape[0] // grp, grp, LANES)
    packed = lambda pat: jnp.broadcast_to(pat, (grp, LANES)).astype(BF16)[None]

    def count_ge(a_ref, pat):
        pb = packed(pat)

        def one(u, acc):
            a = as_groups(a_ref[unit_rows(u), :])
            m = jnp.where(a >= pb, jnp.ones_like(a), jnp.zeros_like(a))
            return acc + _tree_sum([m[i] for i in range(m.shape[0])])

        acc = lax.fori_loop(0, nu // 2, lambda i, acc: one(2 * i + 1, one(2 * i, acc)),
                            jnp.zeros((grp, LANES), BF16))
        acc = lax.cond(nu % 2 == 1, lambda: one(nu - 1, acc), lambda: acc)
        return jnp.sum(acc.astype(F32), axis=0, keepdims=True)

    def digit_search(a_ref, nbits, need):
        def bit(i, carry):
            d, above = carry
            cand = d | (jnp.int32(1) << (nbits - 1 - i))
            cnt = count_ge(a_ref, _pattern(cand))
            ok = cnt >= need
            return jnp.where(ok, cand, d), jnp.where(ok, above, cnt)

        return lax.fori_loop(0, nbits, bit, (jnp.zeros((1, LANES), jnp.int32), jnp.zeros((1, LANES), F32)))

    def narrow(match_ref, digit, src_ref, dst_ref):
        pb = packed(_pattern(digit))

        def body(u, carry):
            mt, sr = as_groups(match_ref[unit_rows(u), :]), as_groups(src_ref[unit_rows(u), :])
            dst_ref[unit_rows(u), :] = jnp.where(mt == pb, sr, jnp.zeros_like(sr)).reshape(UNIT, LANES)
            return carry

        lax.fori_loop(0, nu, body, 0)

    def count_keys(pred):
        def body(u, acc):
            for j in range(UNIT // SUB):
                m = pred(key_ref[sub_rows(u, j), :], key_pos(u, j)).astype(F32)
                acc = acc + jnp.sum(m.reshape(SUB // 8, 8, LANES), axis=0)
            return acc

        return jnp.sum(lax.fori_loop(0, nu, body, jnp.zeros((8, LANES), F32)), axis=0, keepdims=True)

    def write_sel(keep):
        def body(u, carry):
            for j in range(UNIT // SUB):
                rows = sub_rows(u, j)
                sel_ref[rows, :] = (keep(key_ref[rows, :], key_pos(u, j)) & causal(u, j)).astype(F32)
            return carry

        lax.fori_loop(0, nu, body, 0)

    @pl.when(n * BLOCK < TOPK_MAX)
    def _():
        write_sel(lambda k, kp: k == k)

    @pl.when(n * BLOCK >= TOPK_MAX)
    def _():
        need1 = jnp.full((1, LANES), TOPK_MAX, F32)
        d1, above1 = digit_search(atop_ref, 4, need1)
        narrow(atop_ref, d1, amid_ref, a2_ref)
        need2 = need1 - above1
        d2, above2 = digit_search(a2_ref, 14, need2)
        narrow(a2_ref, d2, alow_ref, a3_ref)
        d3, _ = digit_search(a3_ref, 14, need2 - above2)
        thr = ((d1 << 28) | (d2 << 14) | d3) ^ INT_MIN
        write_sel(lambda k, kp: k >= thr)
        nge = count_keys(lambda k, kp: k >= thr)

        @pl.when(jnp.max(nge) > TOPK_MAX)
        def _():
            need = TOPK_MAX - count_keys(lambda k, kp: k > thr)

            def jbit(i, jcut):
                cj = jcut | (jnp.int32(1) << (pos_bits - 1 - i))
                below = count_keys(lambda k, kp: (k == thr) & (kp < cj))
                return jnp.where(below < need, cj, jcut)

            pos_bits = (ki_ref.shape[1] - 1).bit_length()
            jcut = lax.fori_loop(0, pos_bits, jbit, jnp.zeros((1, LANES), jnp.int32))
            jcut = jnp.where(nge > TOPK_MAX, jcut, INT_MAX)
            write_sel(lambda k, kp: (k > thr) | ((k == thr) & (kp <= jcut)))

    def mask_unit(u, macc):
        for j in range(UNIT // SUB):
            rows = sub_rows(u, j)
            sel = sel_ref[rows, :] > 0.0
            mx = []
            for h in range(DSA_Q_HEADS):
                cols = slice(h * LANES, (h + 1) * LANES)
                sh = jnp.where(sel, s_ref[rows, cols], NEG)
                s_ref[rows, cols] = sh
                mx.append(jnp.max(sh.reshape(SUB // 8, 8, LANES), axis=0))
            macc = jnp.maximum(macc, jnp.concatenate(mx, axis=1))
        return macc

    width = DSA_Q_HEADS * LANES
    macc = lax.fori_loop(0, nu, mask_unit, jnp.full((8, width), -jnp.inf, F32))
    m = jnp.max(macc, axis=0, keepdims=True)
    oacc_ref[...] = jnp.zeros(oacc_ref.shape, F32)

    def pv_unit(u, lacc):
        for j in range(UNIT // SUB):
            p = jnp.exp2(s_ref[sub_rows(u, j), :] - m)
            lacc = lacc + jnp.sum(p.reshape(SUB // 8, 8, width), axis=0)
            oacc_ref[...] += jnp.dot(vt_ref[u * (UNIT // SUB) + j], p.astype(BF16),
                                     preferred_element_type=F32)
        return lacc

    lacc = lax.fori_loop(0, nu, pv_unit, jnp.zeros((8, width), F32))
    r = 1.0 / jnp.sum(lacc, axis=0, keepdims=True)
    o = oacc_ref[...]
    row = lax.broadcasted_iota(jnp.int32, (LANES, BLOCK), 0)
    for j in range(DSA_Q_HEADS // 2):
        pair = jnp.where(row < HEAD_DIM, head(o, 2 * j) * head(r, 2 * j),
                         head(o, 2 * j + 1) * head(r, 2 * j + 1))
        o_ref[0, :, j * LANES:(j + 1) * LANES] = pair.T.astype(o_ref.dtype)


def _dsa(qi, wi, qb, ki, kb, vt):
    b, s, _ = qb.shape
    cur = lambda w: pl.BlockSpec((1, BLOCK, w), lambda bi, n: (bi, n, 0))
    seq = pl.BlockSpec((1, s, LANES), lambda bi, n: (bi, 0, 0))
    rows = DSA_Q_HEADS * BLOCK
    digits = pltpu.VMEM((s, LANES), BF16)
    return pl.pallas_call(
        _dsa_kernel,
        grid=(b, s // BLOCK),
        in_specs=[cur(QI_W), cur(LANES), cur(QB_W), seq, seq,
                  pl.BlockSpec((s // SUB, LANES, SUB), lambda bi, n: (bi, 0, 0))],
        out_specs=cur(QB_W),
        out_shape=jax.ShapeDtypeStruct((b, s, QB_W), BF16),
        scratch_shapes=[
            pltpu.VMEM((LANES, IDX_HEADS * BLOCK), BF16),
            pltpu.VMEM((LANES, rows), BF16),
            pltpu.VMEM((s, LANES), jnp.int32),
            digits, digits, digits,
            digits, digits,
            pltpu.VMEM((s, LANES), F32),
            pltpu.VMEM((s, rows), F32),
            pltpu.VMEM((LANES, rows), F32),
        ],
        compiler_params=pltpu.CompilerParams(vmem_limit_bytes=VMEM_LIMIT),
        name="dsa_attn",
    )(qi, wi, qb, ki, kb, vt)


def _out_proj_kernel(alpha, oa_ref, ob_ref, x_ref, wo_ref, g_ref, b_ref, x1_ref, x1b_ref):
    mix = (jnp.dot(oa_ref[...], wo_ref[:QA_W, :], preferred_element_type=F32)
           + jnp.dot(ob_ref[...], wo_ref[QA_W:, :], preferred_element_type=F32))
    y = _ln(alpha * x_ref[...] + mix, g_ref[...], b_ref[...])
    x1_ref[...] = y
    x1b_ref[...] = y.astype(BF16)


def _out_proj(oa, ob, x, wo, g, b, alpha):
    t = x.shape[0]
    tm = 512
    row = lambda w: pl.BlockSpec((tm, w), lambda i: (i, 0))
    const = lambda shape: pl.BlockSpec(shape, lambda i: (0, 0))
    return pl.pallas_call(
        functools.partial(_out_proj_kernel, alpha),
        grid=(t // tm,),
        in_specs=[row(QA_W), row(QB_W), row(D_MODEL), const(wo.shape),
                  const((1, D_MODEL)), const((1, D_MODEL))],
        out_specs=[row(D_MODEL), row(D_MODEL)],
        out_shape=[jax.ShapeDtypeStruct((t, D_MODEL), F32), jax.ShapeDtypeStruct((t, D_MODEL), BF16)],
        compiler_params=pltpu.CompilerParams(vmem_limit_bytes=VMEM_LIMIT),
        name="out_proj_ln",
    )(oa, ob, x, wo, g, b)


def _ffn_kernel(alpha, d_ff, x1_ref, x1b_ref, p_ref, wgu_ref, wd_ref, wpg_ref, wpp_ref, g_ref, b_ref,
                y_ref, yb_ref, h_ref):
    xb = x1b_ref[...]
    fc = 2 * LANES
    for j in range(d_ff // fc):
        g = jnp.dot(xb, wgu_ref[:, j * fc:(j + 1) * fc], preferred_element_type=F32)
        u = jnp.dot(xb, wgu_ref[:, d_ff + j * fc:d_ff + (j + 1) * fc], preferred_element_type=F32)
        h_ref[:, j * fc:(j + 1) * fc] = (g * jax.nn.sigmoid(g) * u).astype(BF16)
    ffn = jnp.dot(h_ref[...], wd_ref[...], preferred_element_type=F32)
    gate = jax.nn.sigmoid(jnp.dot(xb, wpg_ref[...], preferred_element_type=F32))
    ple = jnp.dot(p_ref[...].astype(BF16), wpp_ref[...], preferred_element_type=F32) * gate
    y = _ln(alpha * x1_ref[...] + ffn + ple, g_ref[...], b_ref[...])
    y_ref[...] = y
    yb_ref[...] = y.astype(BF16)


def _ffn(x1, x1b, p, wgu, wd, wpg, wpp, g, b, alpha):
    t = x1.shape[0]
    d_ff = wd.shape[0]
    tm = 512
    row = lambda w: pl.BlockSpec((tm, w), lambda i: (i, 0))
    const = lambda shape: pl.BlockSpec(shape, lambda i: (0, 0), pipeline_mode=pl.Buffered(1))
    return pl.pallas_call(
        functools.partial(_ffn_kernel, alpha, d_ff),
        grid=(t // tm,),
        in_specs=[row(D_MODEL), row(D_MODEL), row(PLE_DIM), const(wgu.shape), const(wd.shape),
                  const(wpg.shape), const(wpp.shape), const((1, D_MODEL)), const((1, D_MODEL))],
        out_specs=[row(D_MODEL), row(D_MODEL)],
        out_shape=[jax.ShapeDtypeStruct((t, D_MODEL), F32), jax.ShapeDtypeStruct((t, D_MODEL), BF16)],
        scratch_shapes=[pltpu.VMEM((tm, d_ff), BF16)],
        compiler_params=pltpu.CompilerParams(vmem_limit_bytes=VMEM_LIMIT),
        name="ffn_ple_ln",
    )(x1, x1b, p, wgu, wd, wpg, wpp, g, b)


def _extend_w_in(w):
    cuts = [0]
    for sz in IN_SIZES:
        cuts.append(cuts[-1] + sz)
    qa, ka, va, qb, kb, vb, qi, ki, wi = [w[:, cuts[i]:cuts[i + 1]] for i in range(len(IN_SIZES))]
    pad = jnp.zeros((w.shape[0], LANES - IDX_HEADS), w.dtype)
    ext = jnp.concatenate([qa, ka, va, qb, kb, kb, vb, vb, qi, ki, ki, ki, ki, wi, pad], axis=1)
    return ext.astype(BF16)


def kernel(x, p, positions, ln_in_g, ln_in_b, w_in, attn_sinks, idx_k_g, idx_k_b, w_o, ln1_g, ln1_b,
           w_gu, w_down, w_pg, w_pp, ln2_g, ln2_b):
    bsz, seq, d = x.shape
    depth = w_in.shape[0]
    t = bsz * seq
    alpha = (2 * depth) ** 0.25
    row = lambda v: v.reshape(1, -1).astype(F32)

    inv64 = ROPE_THETA ** (-jnp.arange(HEAD_DIM // 2, dtype=F32) / (HEAD_DIM // 2))
    inv32 = ROPE_THETA ** (-jnp.arange(IDX_DIM // 2, dtype=F32) / (IDX_DIM // 2))
    inv_row = jnp.concatenate([inv64, inv32, jnp.zeros((LANES - 48,), F32)]).reshape(1, LANES)
    tables = _rope_tables(positions.reshape(t, 1), inv_row)

    xf = x.reshape(t, d)
    xb = None
    for i in range(depth):
        w_ext = _extend_w_in(w_in[i])
        kig = jnp.tile(idx_k_g[i], LANES // IDX_DIM).reshape(1, LANES)
        kib = jnp.tile(idx_k_b[i], LANES // IDX_DIM).reshape(1, LANES)
        if i == 0:
            xf, *segs = _in_proj(xf, w_ext, tables, kig, kib, ln=(row(ln_in_g), row(ln_in_b)))
        else:
            segs = _in_proj(xb, w_ext, tables, kig, kib)
        qa, ka, va, qb, kb, vt, qi, ki, wi = [a if a.ndim == 3 else a.reshape(bsz, seq, -1) for a in segs]
        sink_row = jnp.repeat(attn_sinks[i].astype(F32), BLOCK).reshape(1, -1)
        oa = _swa(qa, ka, va, sink_row).reshape(t, QA_W)
        ob = _dsa(qi, wi, qb, ki, kb, vt).reshape(t, QB_W)
        x1, x1b = _out_proj(oa, ob, xf, w_o[i].astype(BF16), row(ln1_g[i]), row(ln1_b[i]), alpha)
        xf, xb = _ffn(x1, x1b, p[i].reshape(t, PLE_DIM), w_gu[i].astype(BF16), w_down[i].astype(BF16),
                      w_pg[i].astype(BF16), w_pp[i].astype(BF16), row(ln2_g[i]), row(ln2_b[i]), alpha)
    return xf.reshape(bsz, seq, d)
```

```python
import functools

import jax
import jax.numpy as jnp
from jax import lax
from jax.experimental import pallas as pl
from jax.experimental.pallas import tpu as pltpu

D_MODEL = 1024
HEAD_DIM = 64
SWA_Q_HEADS = 8
SWA_KV_HEADS = 2
DSA_Q_HEADS = 8
IDX_HEADS = 8
IDX_DIM = 32
TOPK_MAX = 256
WINDOW = 128
BLOCK = 128
ROPE_THETA = 10000.0
PLE_DIM = 256
LN_EPS = 1e-5
NEG = -1e30
LOG2E = 1.4426950408889634
LANES = 128
QA_W = SWA_Q_HEADS * HEAD_DIM
QB_W = DSA_Q_HEADS * HEAD_DIM
QI_W = IDX_HEADS * IDX_DIM
IN_SIZES = (QA_W, SWA_KV_HEADS * HEAD_DIM, SWA_KV_HEADS * HEAD_DIM, QB_W, HEAD_DIM, HEAD_DIM,
            QI_W, IDX_DIM, IDX_HEADS)
EXT_SEGS = (("qa", QA_W), ("ka", LANES), ("va", LANES), ("qb", QB_W), ("kb", LANES),
            ("vt", LANES), ("qi", QI_W), ("ki", LANES), ("wi", LANES))
EXT_W = sum(w for _, w in EXT_SEGS)
SUB = 256
UNIT = 2 * SUB
INT_MIN = -2 ** 31
INT_MAX = 2 ** 31 - 1
VMEM_LIMIT = 56 * 1024 * 1024
FIELDS = 4
FIELD_BITS = 32 // FIELDS
DIGIT_BITS = (7, 7, 7, 7, 4)
GUARDS = sum(1 << (FIELD_BITS * f + FIELD_BITS - 1) for f in range(FIELDS)) - 2 ** 32
ONES = sum(1 << (FIELD_BITS * f) for f in range(FIELDS))
LOW7 = ONES * 0x7F
PACK = SUB // FIELDS

F32 = jnp.float32
BF16 = jnp.bfloat16
I32 = jnp.int32
srl = lax.shift_right_logical


def _ln(x, g, b):
    mu = jnp.mean(x, axis=-1, keepdims=True)
    xc = x - mu
    var = jnp.mean(xc * xc, axis=-1, keepdims=True)
    return xc * lax.rsqrt(var + LN_EPS) * g + b


def _rope_table_kernel(pos_ref, inv_ref, cos64_ref, sin64_ref, cos32_ref, sin32_ref):
    ang = pos_ref[...].astype(F32) * inv_ref[...]
    lane = lax.broadcasted_iota(jnp.int32, ang.shape, 1)
    c, s = jnp.cos(ang), jnp.sin(ang)

    def tile64(a):
        a = jnp.where(lane < 32, a, pltpu.roll(a, 32, 1))
        return jnp.where(lane < 64, a, pltpu.roll(a, 64, 1))

    def tile32(a):
        a = pltpu.roll(a, 96, 1)
        a = jnp.where(lane < 16, a, pltpu.roll(a, 16, 1))
        a = jnp.where(lane < 32, a, pltpu.roll(a, 32, 1))
        return jnp.where(lane < 64, a, pltpu.roll(a, 64, 1))

    cos64_ref[...] = tile64(c)
    sin64_ref[...] = jnp.where(lane % 64 < 32, -1.0, 1.0) * tile64(s)
    cos32_ref[...] = tile32(c)
    sin32_ref[...] = jnp.where(lane % 32 < 16, -1.0, 1.0) * tile32(s)


def _rope_tables(pos_col, inv_row):
    t = pos_col.shape[0]
    tm = 1024
    row = pl.BlockSpec((tm, LANES), lambda i: (i, 0))
    return pl.pallas_call(
        _rope_table_kernel,
        grid=(t // tm,),
        in_specs=[pl.BlockSpec((tm, 1), lambda i: (i, 0)),
                  pl.BlockSpec((1, LANES), lambda i: (0, 0))],
        out_specs=[row] * 4,
        out_shape=[jax.ShapeDtypeStruct((t, LANES), F32)] * 4,
        name="rope_tables",
    )(pos_col, inv_row)


def _rope_apply(h, cos, sin_signed, half):
    lane = lax.broadcasted_iota(jnp.int32, h.shape, 1)
    partner = jnp.where(lane % (2 * half) < half,
                        pltpu.roll(h, LANES - half, 1), pltpu.roll(h, half, 1))
    return h * cos + partner * sin_signed


def _in_proj_kernel(apply_ln, *refs):
    if apply_ln:
        (x_ref, lng_ref, lnb_ref, w_ref, cos64_ref, sin64_ref, cos32_ref, sin32_ref,
         kig_ref, kib_ref, xn_ref, *outs) = refs
        xn = _ln(x_ref[...], lng_ref[...], lnb_ref[...])
        xn_ref[...] = xn
        xb = xn.astype(BF16)
    else:
        (x_ref, w_ref, cos64_ref, sin64_ref, cos32_ref, sin32_ref,
         kig_ref, kib_ref, *outs) = refs
        xb = x_ref[...]
    out = dict(zip([n for n, _ in EXT_SEGS], outs))
    cos64, sin64 = cos64_ref[...], sin64_ref[...]
    cos32, sin32 = cos32_ref[...], sin32_ref[...]
    chunks = [(name, j) for name, width in EXT_SEGS for j in range(width // LANES)]
    mxu_n = 2 * LANES
    for c0 in range(0, len(chunks), mxu_n // LANES):
        group = chunks[c0:c0 + mxu_n // LANES]
        hh = jnp.dot(xb, w_ref[:, c0 * LANES:(c0 + len(group)) * LANES], preferred_element_type=F32)
        for k, (name, j) in enumerate(group):
            h = hh[:, k * LANES:(k + 1) * LANES]
            if name in ("qa", "qb"):
                h = _rope_apply(h, cos64, sin64, HEAD_DIM // 2) * (HEAD_DIM ** -0.5 * LOG2E)
            elif name in ("ka", "kb"):
                h = _rope_apply(h, cos64, sin64, HEAD_DIM // 2)
            elif name == "qi":
                h = _rope_apply(h, cos32, sin32, IDX_DIM // 2)
            elif name == "ki":
                h = _rope_apply(_ln(h, kig_ref[...], kib_ref[...]), cos32, sin32, IDX_DIM // 2)
            elif name == "wi":
                h = h * (IDX_HEADS ** -0.5 * IDX_DIM ** -0.5)
            if name == "vt":
                for r in range(h.shape[0] // SUB):
                    out[name][r] = h[r * SUB:(r + 1) * SUB].T.astype(BF16)
            else:
                out[name][:, j * LANES:(j + 1) * LANES] = h.astype(out[name].dtype)


def _in_proj(x, w_ext, tables, kig, kib, ln=None):
    t = x.shape[0]
    tm = 512
    row = lambda w: pl.BlockSpec((tm, w), lambda i: (i, 0))
    const = lambda shape: pl.BlockSpec(shape, lambda i: (0, 0))
    in_specs = [row(D_MODEL)]
    args = [x]
    if ln is not None:
        in_specs += [const((1, D_MODEL))] * 2
        args += list(ln)
    in_specs += [const((D_MODEL, EXT_W))] + [row(LANES)] * 4 + [const((1, LANES))] * 2
    args += [w_ext, *tables, kig, kib]
    out_specs, out_shape = [], []
    if ln is not None:
        out_specs.append(row(D_MODEL))
        out_shape.append(jax.ShapeDtypeStruct((t, D_MODEL), F32))
    for name, width in EXT_SEGS:
        if name == "vt":
            out_specs.append(pl.BlockSpec((tm // SUB, LANES, SUB), lambda i: (i, 0, 0)))
            out_shape.append(jax.ShapeDtypeStruct((t // SUB, LANES, SUB), BF16))
            continue
        out_specs.append(row(width))
        out_shape.append(jax.ShapeDtypeStruct((t, width), F32 if name == "wi" else BF16))
    return pl.pallas_call(
        functools.partial(_in_proj_kernel, ln is not None),
        grid=(t // tm,),
        in_specs=in_specs, out_specs=out_specs, out_shape=out_shape,
        compiler_params=pltpu.CompilerParams(vmem_limit_bytes=VMEM_LIMIT),
        name="in_proj",
    )(*args)


def _swa_kernel(q_ref, kc_ref, kp_ref, vc_ref, vp_ref, sink_ref, o_ref):
    n = pl.program_id(1)
    row = lax.broadcasted_iota(jnp.int32, (LANES, BLOCK), 0)
    head = lambda a, h: a[:, h * LANES:(h + 1) * LANES]
    group = SWA_Q_HEADS // SWA_KV_HEADS
    swap = lambda a: pltpu.roll(a, HEAD_DIM, 0)
    q = q_ref[0].astype(F32)
    cols = []
    for j in range(SWA_Q_HEADS // 2):
        qt = head(q, j).T
        kv = (2 * j) // group
        keep = (row < HEAD_DIM) if kv == 0 else (row >= HEAD_DIM)
        even, odd = (qt, swap(qt)) if kv == 0 else (swap(qt), qt)
        cols += [jnp.where(keep, even, 0.0), jnp.where(keep, odd, 0.0)]
    qt = jnp.concatenate(cols, axis=1).astype(BF16)
    kk = jnp.concatenate([kp_ref[0], kc_ref[0]], 0)
    s = jnp.dot(kk, qt, preferred_element_type=F32)
    kidx = lax.broadcasted_iota(jnp.int32, (2 * BLOCK, BLOCK), 0)
    diff = BLOCK + lax.broadcasted_iota(jnp.int32, (2 * BLOCK, BLOCK), 1) - kidx
    first_key = jnp.where(n > 0, 0, BLOCK)
    mask = (diff >= 0) & (diff < WINDOW) & (kidx >= first_key)
    s = jnp.concatenate([jnp.where(mask, head(s, h), NEG) for h in range(SWA_Q_HEADS)], axis=1)
    sink = sink_ref[...] * LOG2E
    m = jnp.maximum(jnp.max(s, axis=0, keepdims=True), sink)
    p = jnp.exp2(s - m)
    r = 1.0 / (jnp.sum(p, axis=0, keepdims=True) + jnp.exp2(sink - m))
    vt = jnp.concatenate([vp_ref[0], vc_ref[0]], 0).astype(F32).T.astype(BF16)
    ot = jnp.dot(vt, p.astype(BF16), preferred_element_type=F32) * r
    for j in range(SWA_Q_HEADS // 2):
        a, b = head(ot, 2 * j), head(ot, 2 * j + 1)
        pair = jnp.where(row < HEAD_DIM, a, swap(b)) if (2 * j) // group == 0 else \
            jnp.where(row < HEAD_DIM, swap(a), b)
        o_ref[0, :, j * LANES:(j + 1) * LANES] = pair.T.astype(o_ref.dtype)


def _swa(qa, ka, va, sink_row):
    b, s, _ = qa.shape
    cur = lambda w: pl.BlockSpec((1, BLOCK, w), lambda bi, n: (bi, n, 0))
    prev = lambda w: pl.BlockSpec((1, BLOCK, w), lambda bi, n: (bi, jnp.maximum(n - 1, 0), 0))
    return pl.pallas_call(
        _swa_kernel,
        grid=(b, s // BLOCK),
        in_specs=[cur(QA_W), cur(LANES), prev(LANES), cur(LANES), prev(LANES),
                  pl.BlockSpec((1, SWA_Q_HEADS * LANES), lambda bi, n: (0, 0))],
        out_specs=cur(QA_W),
        out_shape=jax.ShapeDtypeStruct((b, s, QA_W), BF16),
        name="swa_attn",
    )(qa, ka, ka, va, va, sink_row)


def _tree_sum(parts):
    while len(parts) > 1:
        parts = [a + b for a, b in zip(parts[::2], parts[1::2])] + ([parts[-1]] if len(parts) % 2 else [])
    return parts[0]


def _pack_digits(ukey, shift, bits):
    word = None
    for f in range(FIELDS):
        k = ukey[f * PACK:(f + 1) * PACK]
        pos = FIELD_BITS * f
        k = srl(k, shift - pos) if shift >= pos else k << (pos - shift)
        k = k & (((1 << bits) - 1) << pos)
        word = k if word is None else word | k
    return word | GUARDS


def _field_sum(acc):
    tot = _tree_sum([srl(acc, FIELD_BITS * f) & 0xFF for f in range(FIELDS)])
    return jnp.sum(tot, axis=0, keepdims=True)


def _dsa_kernel(qi_ref, wi_ref, qb_ref, ki_ref, kb_ref, vt_ref, o_ref,
                qim_ref, qs_ref, key_ref, dig_ref, act_ref, live_ref, sel_ref, s_ref, oacc_ref):
    n = pl.program_id(1)
    nu = n // (UNIT // BLOCK) + 1
    lane = lax.broadcasted_iota(jnp.int32, (BLOCK, LANES), 1)
    sub_rows = lambda u, j: pl.ds(pl.multiple_of(u * UNIT + j * SUB, SUB), SUB)
    pack_rows = lambda u, j: pl.ds(pl.multiple_of((u * UNIT + j * SUB) // FIELDS, PACK), PACK)
    pack_unit = lambda u: pl.ds(pl.multiple_of(u * (UNIT // FIELDS), UNIT // FIELDS), UNIT // FIELDS)
    key_pos = lambda u, j: u * UNIT + j * SUB + lax.broadcasted_iota(jnp.int32, (SUB, LANES), 0)
    causal = lambda u, j: key_pos(u, j) <= n * BLOCK + lax.broadcasted_iota(jnp.int32, (SUB, LANES), 1)
    head = lambda a, h: a[:, h * LANES:(h + 1) * LANES]
    words = lambda a: a.reshape(a.shape[0] // 8, 8, LANES)

    def each_unit(one, init):
        carry = lax.fori_loop(0, nu // 2, lambda i, c: one(2 * i + 1, one(2 * i, c)), init)
        return lax.cond(nu % 2 == 1, lambda: one(nu - 1, carry), lambda: carry)

    qi = qi_ref[0].astype(F32)
    for h in range(IDX_HEADS):
        per = LANES // IDX_DIM
        qim_ref[:, h * BLOCK:(h + 1) * BLOCK] = jnp.where(
            lane // IDX_DIM == h % per, head(qi, h // per), 0.0).T.astype(BF16)
    qb = qb_ref[0].astype(F32)
    for h in range(DSA_Q_HEADS):
        keep = (lane < HEAD_DIM) if h % 2 == 0 else (lane >= HEAD_DIM)
        qs_ref[:, h * BLOCK:(h + 1) * BLOCK] = jnp.where(keep, head(qb, h // 2), 0.0).T.astype(BF16)
    wt = wi_ref[0].T
    w_rows = [wt[h:h + 1, :] for h in range(IDX_HEADS)]

    shifts = [sum(DIGIT_BITS[i + 1:]) for i in range(len(DIGIT_BITS))]

    def idx_unit(u, carry):
        for j in range(UNIT // SUB):
            rows = sub_rows(u, j)
            sc = jnp.dot(ki_ref[0, rows, :], qim_ref[...], preferred_element_type=F32)
            acc = jnp.maximum(head(sc, 0), 0.0) * w_rows[0]
            for h in range(1, IDX_HEADS):
                acc = acc + jnp.maximum(head(sc, h), 0.0) * w_rows[h]
            acc = jnp.where(acc == 0.0, 0.0, acc)
            idx = jnp.where(causal(u, j), acc, NEG)
            bits = lax.bitcast_convert_type(idx, I32)
            key = bits ^ ((bits >> 31) & INT_MAX)
            key_ref[rows, :] = key
            ukey = key ^ INT_MIN
            for p, (shift, nbits) in enumerate(zip(shifts, DIGIT_BITS)):
                dig_ref[p, pack_rows(u, j), :] = _pack_digits(ukey, shift, nbits)
        return carry

    lax.fori_loop(0, nu, idx_unit, 0)

    def count_ge(cand):
        crep = jnp.broadcast_to(cand * ONES, (8, LANES))

        def one(u, acc):
            a = words(act_ref[pack_unit(u), :])
            t = srl(a - crep[None], FIELD_BITS - 1) & ONES
            return acc + _tree_sum([t[i] for i in range(t.shape[0])])

        return _field_sum(each_unit(one, jnp.zeros((8, LANES), I32)))

    def digit_search(nbits, need):
        def bit(i, carry):
            d, above = carry
            cand = d | (jnp.int32(1) << (nbits - 1 - i))
            cnt = count_ge(cand)
            ok = cnt >= need
            return jnp.where(ok, cand, d), jnp.where(ok, above, cnt)

        return lax.fori_loop(0, nbits, bit, (jnp.zeros((1, LANES), I32), jnp.zeros((1, LANES), I32)))

    def narrow(p, digit, first):
        want = jnp.broadcast_to((digit * ONES) | GUARDS, (8, LANES))[None]

        def one(u, carry):
            rows = pack_unit(u)
            e = words(dig_ref[p, rows, :]) ^ want
            nonzero = (((e & LOW7) + LOW7) | e) & GUARDS
            zero = nonzero ^ GUARDS
            match = zero | (zero - srl(zero, FIELD_BITS - 1))
            if not first:
                match = match & words(live_ref[rows, :])
            live_ref[rows, :] = match.reshape(UNIT // FIELDS, LANES)
            nxt = (words(dig_ref[p + 1, rows, :]) & match) | GUARDS
            act_ref[rows, :] = nxt.reshape(UNIT // FIELDS, LANES)
            return carry

        each_unit(one, 0)

    def count_keys(pred):
        def one(u, acc):
            for j in range(UNIT // SUB):
                m = pred(key_ref[sub_rows(u, j), :], key_pos(u, j)).astype(I32)
                acc = acc + jnp.sum(m.reshape(SUB // 8, 8, LANES), axis=0)
            return acc

        return jnp.sum(each_unit(one, jnp.zeros((8, LANES), I32)), axis=0, keepdims=True)

    def write_sel(keep):
        def one(u, acc):
            for j in range(UNIT // SUB):
                rows = sub_rows(u, j)
                m = keep(key_ref[rows, :], key_pos(u, j))
                sel_ref[rows, :] = (m & causal(u, j)).astype(F32)
                acc = acc + jnp.sum(m.astype(I32).reshape(SUB // 8, 8, LANES), axis=0)
            return acc

        return jnp.sum(each_unit(one, jnp.zeros((8, LANES), I32)), axis=0, keepdims=True)

    @pl.when(n * BLOCK < TOPK_MAX)
    def _():
        write_sel(lambda k, kp: k == k)

    @pl.when(n * BLOCK >= TOPK_MAX)
    def _():
        def copy_first(u, carry):
            act_ref[pack_unit(u), :] = dig_ref[0, pack_unit(u), :]
            return carry

        each_unit(copy_first, 0)
        need = jnp.full((1, LANES), TOPK_MAX, I32)
        ukey_thr = jnp.zeros((1, LANES), I32)
        for p, nbits in enumerate(DIGIT_BITS):
            d, above = digit_search(nbits, need)
            ukey_thr = (ukey_thr << nbits) | d
            need = need - above
            if p + 1 < len(DIGIT_BITS):
                narrow(p, d, first=(p == 0))
        thr = ukey_thr ^ INT_MIN
        nge = write_sel(lambda k, kp: k >= thr)

        @pl.when(jnp.max(nge) > TOPK_MAX)
        def _():
            missing = TOPK_MAX - count_keys(lambda k, kp: k > thr)
            pos_bits = (ki_ref.shape[1] - 1).bit_length()

            def jbit(i, jcut):
                cj = jcut | (jnp.int32(1) << (pos_bits - 1 - i))
                below = count_keys(lambda k, kp: (k == thr) & (kp < cj))
                return jnp.where(below < missing, cj, jcut)

            jcut = lax.fori_loop(0, pos_bits, jbit, jnp.zeros((1, LANES), I32))
            jcut = jnp.where(nge > TOPK_MAX, jcut, INT_MAX)
            write_sel(lambda k, kp: (k > thr) | ((k == thr) & (kp <= jcut)))

    def mask_unit(u, macc):
        for j in range(UNIT // SUB):
            rows = sub_rows(u, j)
            sel = sel_ref[rows, :] > 0.0
            s = jnp.dot(kb_ref[0, rows, :], qs_ref[...], preferred_element_type=F32)
            mx = []
            for h in range(DSA_Q_HEADS):
                cols = slice(h * LANES, (h + 1) * LANES)
                sh = jnp.where(sel, head(s, h), NEG)
                s_ref[rows, cols] = sh
                mx.append(jnp.max(sh.reshape(SUB // 8, 8, LANES), axis=0))
            macc = jnp.maximum(macc, jnp.concatenate(mx, axis=1))
        return macc

    width = DSA_Q_HEADS * LANES
    macc = lax.fori_loop(0, nu, mask_unit, jnp.full((8, width), -jnp.inf, F32))
    m = jnp.max(macc, axis=0, keepdims=True)
    oacc_ref[...] = jnp.zeros(oacc_ref.shape, F32)

    def pv_unit(u, lacc):
        for j in range(UNIT // SUB):
            p = jnp.exp2(s_ref[sub_rows(u, j), :] - m)
            lacc = lacc + jnp.sum(p.reshape(SUB // 8, 8, width), axis=0)
            oacc_ref[...] += jnp.dot(vt_ref[u * (UNIT // SUB) + j], p.astype(BF16),
                                     preferred_element_type=F32)
        return lacc

    lacc = lax.fori_loop(0, nu, pv_unit, jnp.zeros((8, width), F32))
    r = 1.0 / jnp.sum(lacc, axis=0, keepdims=True)
    o = oacc_ref[...]
    row = lax.broadcasted_iota(jnp.int32, (LANES, BLOCK), 0)
    for j in range(DSA_Q_HEADS // 2):
        pair = jnp.where(row < HEAD_DIM, head(o, 2 * j) * head(r, 2 * j),
                         head(o, 2 * j + 1) * head(r, 2 * j + 1))
        o_ref[0, :, j * LANES:(j + 1) * LANES] = pair.T.astype(o_ref.dtype)


def _dsa(qi, wi, qb, ki, kb, vt):
    b, s, _ = qb.shape
    cur = lambda w: pl.BlockSpec((1, BLOCK, w), lambda bi, n: (bi, n, 0))
    seq = pl.BlockSpec((1, s, LANES), lambda bi, n: (bi, 0, 0))
    rows = DSA_Q_HEADS * BLOCK
    return pl.pallas_call(
        _dsa_kernel,
        grid=(b, s // BLOCK),
        in_specs=[cur(QI_W), cur(LANES), cur(QB_W), seq, seq,
                  pl.BlockSpec((s // SUB, LANES, SUB), lambda bi, n: (bi, 0, 0))],
        out_specs=cur(QB_W),
        out_shape=jax.ShapeDtypeStruct((b, s, QB_W), BF16),
        scratch_shapes=[
            pltpu.VMEM((LANES, IDX_HEADS * BLOCK), BF16),
            pltpu.VMEM((LANES, rows), BF16),
            pltpu.VMEM((s, LANES), I32),
            pltpu.VMEM((len(DIGIT_BITS), s // FIELDS, LANES), I32),
            pltpu.VMEM((s // FIELDS, LANES), I32),
            pltpu.VMEM((s // FIELDS, LANES), I32),
            pltpu.VMEM((s, LANES), F32),
            pltpu.VMEM((s, rows), F32),
            pltpu.VMEM((LANES, rows), F32),
        ],
        compiler_params=pltpu.CompilerParams(vmem_limit_bytes=VMEM_LIMIT),
        name="dsa_attn",
    )(qi, wi, qb, ki, kb, vt)


def _out_proj_kernel(alpha, oa_ref, ob_ref, x_ref, wo_ref, g_ref, b_ref, x1_ref, x1b_ref):
    mix = (jnp.dot(oa_ref[...], wo_ref[:QA_W, :], preferred_element_type=F32)
           + jnp.dot(ob_ref[...], wo_ref[QA_W:, :], preferred_element_type=F32))
    y = _ln(alpha * x_ref[...] + mix, g_ref[...], b_ref[...])
    x1_ref[...] = y
    x1b_ref[...] = y.astype(BF16)


def _out_proj(oa, ob, x, wo, g, b, alpha):
    t = x.shape[0]
    tm = 512
    row = lambda w: pl.BlockSpec((tm, w), lambda i: (i, 0))
    const = lambda shape: pl.BlockSpec(shape, lambda i: (0, 0))
    return pl.pallas_call(
        functools.partial(_out_proj_kernel, alpha),
        grid=(t // tm,),
        in_specs=[row(QA_W), row(QB_W), row(D_MODEL), const(wo.shape),
                  const((1, D_MODEL)), const((1, D_MODEL))],
        out_specs=[row(D_MODEL), row(D_MODEL)],
        out_shape=[jax.ShapeDtypeStruct((t, D_MODEL), F32), jax.ShapeDtypeStruct((t, D_MODEL), BF16)],
        compiler_params=pltpu.CompilerParams(vmem_limit_bytes=VMEM_LIMIT),
        name="out_proj_ln",
    )(oa, ob, x, wo, g, b)


def _ffn_kernel(alpha, d_ff, x1_ref, x1b_ref, p_ref, wgu_ref, wd_ref, wpg_ref, wpp_ref, g_ref, b_ref,
                y_ref, yb_ref, h_ref):
    xb = x1b_ref[...]
    fc = 2 * LANES
    for j in range(d_ff // fc):
        g = jnp.dot(xb, wgu_ref[:, j * fc:(j + 1) * fc], preferred_element_type=F32)
        u = jnp.dot(xb, wgu_ref[:, d_ff + j * fc:d_ff + (j + 1) * fc], preferred_element_type=F32)
        h_ref[:, j * fc:(j + 1) * fc] = (g * jax.nn.sigmoid(g) * u).astype(BF16)
    ffn = jnp.dot(h_ref[...], wd_ref[...], preferred_element_type=F32)
    gate = jax.nn.sigmoid(jnp.dot(xb, wpg_ref[...], preferred_element_type=F32))
    ple = jnp.dot(p_ref[...].astype(BF16), wpp_ref[...], preferred_element_type=F32) * gate
    y = _ln(alpha * x1_ref[...] + ffn + ple, g_ref[...], b_ref[...])
    y_ref[...] = y
    yb_ref[...] = y.astype(BF16)


def _ffn(x1, x1b, p, wgu, wd, wpg, wpp, g, b, alpha):
    t = x1.shape[0]
    d_ff = wd.shape[0]
    tm = 512
    row = lambda w: pl.BlockSpec((tm, w), lambda i: (i, 0))
    const = lambda shape: pl.BlockSpec(shape, lambda i: (0, 0), pipeline_mode=pl.Buffered(1))
    return pl.pallas_call(
        functools.partial(_ffn_kernel, alpha, d_ff),
        grid=(t // tm,),
        in_specs=[row(D_MODEL), row(D_MODEL), row(PLE_DIM), const(wgu.shape), const(wd.shape),
                  const(wpg.shape), const(wpp.shape), const((1, D_MODEL)), const((1, D_MODEL))],
        out_specs=[row(D_MODEL), row(D_MODEL)],
        out_shape=[jax.ShapeDtypeStruct((t, D_MODEL), F32), jax.ShapeDtypeStruct((t, D_MODEL), BF16)],
        scratch_shapes=[pltpu.VMEM((tm, d_ff), BF16)],
        compiler_params=pltpu.CompilerParams(vmem_limit_bytes=VMEM_LIMIT),
        name="ffn_ple_ln",
    )(x1, x1b, p, wgu, wd, wpg, wpp, g, b)


def _extend_w_in(w):
    cuts = [0]
    for sz in IN_SIZES:
        cuts.append(cuts[-1] + sz)
    qa, ka, va, qb, kb, vb, qi, ki, wi = [w[:, cuts[i]:cuts[i + 1]] for i in range(len(IN_SIZES))]
    pad = jnp.zeros((w.shape[0], LANES - IDX_HEADS), w.dtype)
    ext = jnp.concatenate([qa, ka, va, qb, kb, kb, vb, vb, qi, ki, ki, ki, ki, wi, pad], axis=1)
    return ext.astype(BF16)


def kernel(x, p, positions, ln_in_g, ln_in_b, w_in, attn_sinks, idx_k_g, idx_k_b, w_o, ln1_g, ln1_b,
           w_gu, w_down, w_pg, w_pp, ln2_g, ln2_b):
    bsz, seq, d = x.shape
    depth = w_in.shape[0]
    t = bsz * seq
    alpha = (2 * depth) ** 0.25
    row = lambda v: v.reshape(1, -1).astype(F32)

    inv64 = ROPE_THETA ** (-jnp.arange(HEAD_DIM // 2, dtype=F32) / (HEAD_DIM // 2))
    inv32 = ROPE_THETA ** (-jnp.arange(IDX_DIM // 2, dtype=F32) / (IDX_DIM // 2))
    inv_row = jnp.concatenate([inv64, inv32, jnp.zeros((LANES - 48,), F32)]).reshape(1, LANES)
    tables = _rope_tables(positions.reshape(t, 1), inv_row)

    xf = x.reshape(t, d)
    xb = None
    for i in range(depth):
        w_ext = _extend_w_in(w_in[i])
        kig = jnp.tile(idx_k_g[i], LANES // IDX_DIM).reshape(1, LANES)
        kib = jnp.tile(idx_k_b[i], LANES // IDX_DIM).reshape(1, LANES)
        if i == 0:
            xf, *segs = _in_proj(xf, w_ext, tables, kig, kib, ln=(row(ln_in_g), row(ln_in_b)))
        else:
            segs = _in_proj(xb, w_ext, tables, kig, kib)
        qa, ka, va, qb, kb, vt, qi, ki, wi = [a if a.ndim == 3 else a.reshape(bsz, seq, -1) for a in segs]
        sink_row = jnp.repeat(attn_sinks[i].astype(F32), BLOCK).reshape(1, -1)
        oa = _swa(qa, ka, va, sink_row).reshape(t, QA_W)
        ob = _dsa(qi, wi, qb, ki, kb, vt).reshape(t, QB_W)
        x1, x1b = _out_proj(oa, ob, xf, w_o[i].astype(BF16), row(ln1_g[i]), row(ln1_b[i]), alpha)
        xf, xb = _ffn(x1, x1b, p[i].reshape(t, PLE_DIM), w_gu[i].astype(BF16), w_down[i].astype(BF16),
                      w_pg[i].astype(BF16), w_pp[i].astype(BF16), row(ln2_g[i]), row(ln2_b[i]), alpha)
    return xf.reshape(bsz, seq, d)
```

```python
import functools

import jax
import jax.numpy as jnp
from jax import lax
from jax.experimental import pallas as pl
from jax.experimental.pallas import tpu as pltpu

D_MODEL = 1024
HEAD_DIM = 64
SWA_Q_HEADS = 8
SWA_KV_HEADS = 2
DSA_Q_HEADS = 8
IDX_HEADS = 8
IDX_DIM = 32
TOPK_MAX = 256
WINDOW = 128
BLOCK = 128
ROPE_THETA = 10000.0
PLE_DIM = 256
LN_EPS = 1e-5
NEG = -1e30
LOG2E = 1.4426950408889634
LANES = 128
QA_W = SWA_Q_HEADS * HEAD_DIM
QB_W = DSA_Q_HEADS * HEAD_DIM
QI_W = IDX_HEADS * IDX_DIM
IN_SIZES = (QA_W, SWA_KV_HEADS * HEAD_DIM, SWA_KV_HEADS * HEAD_DIM, QB_W, HEAD_DIM, HEAD_DIM,
            QI_W, IDX_DIM, IDX_HEADS)
EXT_SEGS = (("qa", QA_W), ("ka", LANES), ("va", LANES), ("qb", QB_W), ("kb", LANES),
            ("vt", LANES), ("qi", QI_W), ("ki", LANES), ("wi", LANES))
EXT_W = sum(w for _, w in EXT_SEGS)
SUB = 256
UNIT = 2 * SUB
INT_MIN = -2 ** 31
INT_MAX = 2 ** 31 - 1
VMEM_LIMIT = 56 * 1024 * 1024
FIELDS = 4
FIELD_BITS = 32 // FIELDS
DIGIT_BITS = (7, 7, 7, 7, 4)
GUARDS = sum(1 << (FIELD_BITS * f + FIELD_BITS - 1) for f in range(FIELDS)) - 2 ** 32
ONES = sum(1 << (FIELD_BITS * f) for f in range(FIELDS))
LOW7 = ONES * 0x7F
PACK = SUB // FIELDS

F32 = jnp.float32
BF16 = jnp.bfloat16
I32 = jnp.int32
srl = lax.shift_right_logical


def _ln(x, g, b):
    mu = jnp.mean(x, axis=-1, keepdims=True)
    xc = x - mu
    var = jnp.mean(xc * xc, axis=-1, keepdims=True)
    return xc * lax.rsqrt(var + LN_EPS) * g + b


def _rope_table_kernel(pos_ref, inv_ref, cos64_ref, sin64_ref, cos32_ref, sin32_ref):
    ang = pos_ref[...].astype(F32) * inv_ref[...]
    lane = lax.broadcasted_iota(jnp.int32, ang.shape, 1)
    c, s = jnp.cos(ang), jnp.sin(ang)

    def tile64(a):
        a = jnp.where(lane < 32, a, pltpu.roll(a, 32, 1))
        return jnp.where(lane < 64, a, pltpu.roll(a, 64, 1))

    def tile32(a):
        a = pltpu.roll(a, 96, 1)
        a = jnp.where(lane < 16, a, pltpu.roll(a, 16, 1))
        a = jnp.where(lane < 32, a, pltpu.roll(a, 32, 1))
        return jnp.where(lane < 64, a, pltpu.roll(a, 64, 1))

    cos64_ref[...] = tile64(c)
    sin64_ref[...] = jnp.where(lane % 64 < 32, -1.0, 1.0) * tile64(s)
    cos32_ref[...] = tile32(c)
    sin32_ref[...] = jnp.where(lane % 32 < 16, -1.0, 1.0) * tile32(s)


def _rope_tables(pos_col, inv_row):
    t = pos_col.shape[0]
    tm = 1024
    row = pl.BlockSpec((tm, LANES), lambda i: (i, 0))
    return pl.pallas_call(
        _rope_table_kernel,
        grid=(t // tm,),
        in_specs=[pl.BlockSpec((tm, 1), lambda i: (i, 0)),
                  pl.BlockSpec((1, LANES), lambda i: (0, 0))],
        out_specs=[row] * 4,
        out_shape=[jax.ShapeDtypeStruct((t, LANES), F32)] * 4,
        name="rope_tables",
    )(pos_col, inv_row)


def _rope_apply(h, cos, sin_signed, half):
    lane = lax.broadcasted_iota(jnp.int32, h.shape, 1)
    partner = jnp.where(lane % (2 * half) < half,
                        pltpu.roll(h, LANES - half, 1), pltpu.roll(h, half, 1))
    return h * cos + partner * sin_signed


def _in_proj_kernel(apply_ln, *refs):
    if apply_ln:
        (x_ref, lng_ref, lnb_ref, w_ref, cos64_ref, sin64_ref, cos32_ref, sin32_ref,
         kig_ref, kib_ref, xn_ref, *outs) = refs
        xn = _ln(x_ref[...], lng_ref[...], lnb_ref[...])
        xn_ref[...] = xn
        xb = xn.astype(BF16)
    else:
        (x_ref, w_ref, cos64_ref, sin64_ref, cos32_ref, sin32_ref,
         kig_ref, kib_ref, *outs) = refs
        xb = x_ref[...]
    out = dict(zip([n for n, _ in EXT_SEGS], outs))
    cos64, sin64 = cos64_ref[...], sin64_ref[...]
    cos32, sin32 = cos32_ref[...], sin32_ref[...]
    chunks = [(name, j) for name, width in EXT_SEGS for j in range(width // LANES)]
    mxu_n = 2 * LANES
    for c0 in range(0, len(chunks), mxu_n // LANES):
        group = chunks[c0:c0 + mxu_n // LANES]
        hh = jnp.dot(xb, w_ref[:, c0 * LANES:(c0 + len(group)) * LANES], preferred_element_type=F32)
        for k, (name, j) in enumerate(group):
            h = hh[:, k * LANES:(k + 1) * LANES]
            if name in ("qa", "qb"):
                h = _rope_apply(h, cos64, sin64, HEAD_DIM // 2) * (HEAD_DIM ** -0.5 * LOG2E)
            elif name in ("ka", "kb"):
                h = _rope_apply(h, cos64, sin64, HEAD_DIM // 2)
            elif name == "qi":
                h = _rope_apply(h, cos32, sin32, IDX_DIM // 2)
            elif name == "ki":
                h = _rope_apply(_ln(h, kig_ref[...], kib_ref[...]), cos32, sin32, IDX_DIM // 2)
            elif name == "wi":
                h = h * (IDX_HEADS ** -0.5 * IDX_DIM ** -0.5)
            if name == "vt":
                h = jnp.where(lax.broadcasted_iota(jnp.int32, h.shape, 1) < HEAD_DIM, h, 1.0)
                for r in range(h.shape[0] // SUB):
                    out[name][r] = h[r * SUB:(r + 1) * SUB].T.astype(BF16)
            else:
                out[name][:, j * LANES:(j + 1) * LANES] = h.astype(out[name].dtype)


def _in_proj(x, w_ext, tables, kig, kib, ln=None):
    t = x.shape[0]
    tm = 512
    row = lambda w: pl.BlockSpec((tm, w), lambda i: (i, 0))
    const = lambda shape: pl.BlockSpec(shape, lambda i: (0, 0))
    in_specs = [row(D_MODEL)]
    args = [x]
    if ln is not None:
        in_specs += [const((1, D_MODEL))] * 2
        args += list(ln)
    in_specs += [const((D_MODEL, EXT_W))] + [row(LANES)] * 4 + [const((1, LANES))] * 2
    args += [w_ext, *tables, kig, kib]
    out_specs, out_shape = [], []
    if ln is not None:
        out_specs.append(row(D_MODEL))
        out_shape.append(jax.ShapeDtypeStruct((t, D_MODEL), F32))
    for name, width in EXT_SEGS:
        if name == "vt":
            out_specs.append(pl.BlockSpec((tm // SUB, LANES, SUB), lambda i: (i, 0, 0)))
            out_shape.append(jax.ShapeDtypeStruct((t // SUB, LANES, SUB), BF16))
            continue
        out_specs.append(row(width))
        out_shape.append(jax.ShapeDtypeStruct((t, width), F32 if name == "wi" else BF16))
    return pl.pallas_call(
        functools.partial(_in_proj_kernel, ln is not None),
        grid=(t // tm,),
        in_specs=in_specs, out_specs=out_specs, out_shape=out_shape,
        compiler_params=pltpu.CompilerParams(vmem_limit_bytes=VMEM_LIMIT),
        name="in_proj",
    )(*args)


def _swa_kernel(q_ref, kc_ref, kp_ref, vc_ref, vp_ref, sink_ref, o_ref):
    n = pl.program_id(1)
    row = lax.broadcasted_iota(jnp.int32, (LANES, BLOCK), 0)
    head = lambda a, h: a[:, h * LANES:(h + 1) * LANES]
    group = SWA_Q_HEADS // SWA_KV_HEADS
    swap = lambda a: pltpu.roll(a, HEAD_DIM, 0)
    q = q_ref[0].astype(F32)
    cols = []
    for j in range(SWA_Q_HEADS // 2):
        qt = head(q, j).T
        kv = (2 * j) // group
        keep = (row < HEAD_DIM) if kv == 0 else (row >= HEAD_DIM)
        even, odd = (qt, swap(qt)) if kv == 0 else (swap(qt), qt)
        cols += [jnp.where(keep, even, 0.0), jnp.where(keep, odd, 0.0)]
    qt = jnp.concatenate(cols, axis=1).astype(BF16)
    kk = jnp.concatenate([kp_ref[0], kc_ref[0]], 0)
    s = jnp.dot(kk, qt, preferred_element_type=F32)
    kidx = lax.broadcasted_iota(jnp.int32, (2 * BLOCK, BLOCK), 0)
    diff = BLOCK + lax.broadcasted_iota(jnp.int32, (2 * BLOCK, BLOCK), 1) - kidx
    first_key = jnp.where(n > 0, 0, BLOCK)
    mask = (diff >= 0) & (diff < WINDOW) & (kidx >= first_key)
    s = jnp.concatenate([jnp.where(mask, head(s, h), NEG) for h in range(SWA_Q_HEADS)], axis=1)
    sink = sink_ref[...] * LOG2E
    m = jnp.maximum(jnp.max(s, axis=0, keepdims=True), sink)
    p = jnp.exp2(s - m)
    r = 1.0 / (jnp.sum(p, axis=0, keepdims=True) + jnp.exp2(sink - m))
    vt = jnp.concatenate([vp_ref[0], vc_ref[0]], 0).astype(F32).T.astype(BF16)
    ot = jnp.dot(vt, p.astype(BF16), preferred_element_type=F32) * r
    for j in range(SWA_Q_HEADS // 2):
        a, b = head(ot, 2 * j), head(ot, 2 * j + 1)
        pair = jnp.where(row < HEAD_DIM, a, swap(b)) if (2 * j) // group == 0 else \
            jnp.where(row < HEAD_DIM, swap(a), b)
        o_ref[0, :, j * LANES:(j + 1) * LANES] = pair.T.astype(o_ref.dtype)


def _swa(qa, ka, va, sink_row):
    b, s, _ = qa.shape
    cur = lambda w: pl.BlockSpec((1, BLOCK, w), lambda bi, n: (bi, n, 0))
    prev = lambda w: pl.BlockSpec((1, BLOCK, w), lambda bi, n: (bi, jnp.maximum(n - 1, 0), 0))
    return pl.pallas_call(
        _swa_kernel,
        grid=(b, s // BLOCK),
        in_specs=[cur(QA_W), cur(LANES), prev(LANES), cur(LANES), prev(LANES),
                  pl.BlockSpec((1, SWA_Q_HEADS * LANES), lambda bi, n: (0, 0))],
        out_specs=cur(QA_W),
        out_shape=jax.ShapeDtypeStruct((b, s, QA_W), BF16),
        name="swa_attn",
    )(qa, ka, ka, va, va, sink_row)


def _tree_sum(parts):
    while len(parts) > 1:
        parts = [a + b for a, b in zip(parts[::2], parts[1::2])] + ([parts[-1]] if len(parts) % 2 else [])
    return parts[0]


def _pack_digits(ukey, shift, bits):
    word = None
    for f in range(FIELDS):
        k = ukey[f * PACK:(f + 1) * PACK]
        pos = FIELD_BITS * f
        k = srl(k, shift - pos) if shift >= pos else k << (pos - shift)
        k = k & (((1 << bits) - 1) << pos)
        word = k if word is None else word | k
    return word | GUARDS


def _field_sum(acc):
    tot = _tree_sum([srl(acc, FIELD_BITS * f) & 0xFF for f in range(FIELDS)])
    return jnp.sum(tot, axis=0, keepdims=True)


def _dsa_kernel(qi_ref, wi_ref, qb_ref, ki_ref, kb_ref, vt_ref, o_ref,
                qim_ref, qs_ref, key_ref, dig_ref, act_ref, live_ref, sel_ref, s_ref, oacc_ref):
    n = pl.program_id(1)
    nu = n // (UNIT // BLOCK) + 1
    lane = lax.broadcasted_iota(jnp.int32, (BLOCK, LANES), 1)
    sub_rows = lambda u, j: pl.ds(pl.multiple_of(u * UNIT + j * SUB, SUB), SUB)
    pack_rows = lambda u, j: pl.ds(pl.multiple_of((u * UNIT + j * SUB) // FIELDS, PACK), PACK)
    pack_unit = lambda u: pl.ds(pl.multiple_of(u * (UNIT // FIELDS), UNIT // FIELDS), UNIT // FIELDS)
    key_pos = lambda u, j: u * UNIT + j * SUB + lax.broadcasted_iota(jnp.int32, (SUB, LANES), 0)
    causal = lambda u, j: key_pos(u, j) <= n * BLOCK + lax.broadcasted_iota(jnp.int32, (SUB, LANES), 1)
    head = lambda a, h: a[:, h * LANES:(h + 1) * LANES]
    words = lambda a: a.reshape(a.shape[0] // 8, 8, LANES)

    def each_unit(one, init):
        carry = lax.fori_loop(0, nu // 2, lambda i, c: one(2 * i + 1, one(2 * i, c)), init)
        return lax.cond(nu % 2 == 1, lambda: one(nu - 1, carry), lambda: carry)

    qi = qi_ref[0].astype(F32)
    for h in range(IDX_HEADS):
        per = LANES // IDX_DIM
        qim_ref[:, h * BLOCK:(h + 1) * BLOCK] = jnp.where(
            lane // IDX_DIM == h % per, head(qi, h // per), 0.0).T.astype(BF16)
    qb = qb_ref[0].astype(F32)
    for h in range(DSA_Q_HEADS):
        keep = (lane < HEAD_DIM) if h % 2 == 0 else (lane >= HEAD_DIM)
        qs_ref[:, h * BLOCK:(h + 1) * BLOCK] = jnp.where(keep, head(qb, h // 2), 0.0).T.astype(BF16)
    wt = wi_ref[0].T
    w_rows = [wt[h:h + 1, :] for h in range(IDX_HEADS)]

    shifts = [sum(DIGIT_BITS[i + 1:]) for i in range(len(DIGIT_BITS))]

    def idx_unit(u, carry):
        for j in range(UNIT // SUB):
            rows = sub_rows(u, j)
            sc = jnp.dot(ki_ref[0, rows, :], qim_ref[...], preferred_element_type=F32)
            acc = jnp.maximum(head(sc, 0), 0.0) * w_rows[0]
            for h in range(1, IDX_HEADS):
                acc = acc + jnp.maximum(head(sc, h), 0.0) * w_rows[h]
            acc = jnp.where(acc == 0.0, 0.0, acc)
            idx = jnp.where(causal(u, j), acc, NEG)
            bits = lax.bitcast_convert_type(idx, I32)
            key = bits ^ ((bits >> 31) & INT_MAX)
            key_ref[rows, :] = key
            ukey = key ^ INT_MIN
            for p, (shift, nbits) in enumerate(zip(shifts, DIGIT_BITS)):
                dig_ref[p, pack_rows(u, j), :] = _pack_digits(ukey, shift, nbits)
        return carry

    lax.fori_loop(0, nu, idx_unit, 0)

    def count_ge(cand):
        crep = jnp.broadcast_to(cand * ONES, (8, LANES))

        def one(u, acc):
            a = words(act_ref[pack_unit(u), :])
            t = srl(a - crep[None], FIELD_BITS - 1) & ONES
            return acc + _tree_sum([t[i] for i in range(t.shape[0])])

        return _field_sum(each_unit(one, jnp.zeros((8, LANES), I32)))

    def digit_search(nbits, need):
        def bit(i, carry):
            d, above = carry
            cand = d | (jnp.int32(1) << (nbits - 1 - i))
            cnt = count_ge(cand)
            ok = cnt >= need
            return jnp.where(ok, cand, d), jnp.where(ok, above, cnt)

        return lax.fori_loop(0, nbits, bit, (jnp.zeros((1, LANES), I32), jnp.zeros((1, LANES), I32)))

    def narrow(p, digit, first):
        want = jnp.broadcast_to((digit * ONES) | GUARDS, (8, LANES))[None]

        def one(u, carry):
            rows = pack_unit(u)
            e = words(dig_ref[p, rows, :]) ^ want
            nonzero = (((e & LOW7) + LOW7) | e) & GUARDS
            zero = nonzero ^ GUARDS
            match = zero | (zero - srl(zero, FIELD_BITS - 1))
            if not first:
                match = match & words(live_ref[rows, :])
            live_ref[rows, :] = match.reshape(UNIT // FIELDS, LANES)
            nxt = (words(dig_ref[p + 1, rows, :]) & match) | GUARDS
            act_ref[rows, :] = nxt.reshape(UNIT // FIELDS, LANES)
            return carry

        each_unit(one, 0)

    def write_sel(keep):
        def one(u, acc):
            for j in range(UNIT // SUB):
                rows = sub_rows(u, j)
                m = keep(key_ref[rows, :], key_pos(u, j))
                sel_ref[rows, :] = (m & causal(u, j)).astype(F32)
                acc = acc + jnp.sum(m.astype(I32).reshape(SUB // 8, 8, LANES), axis=0)
            return acc

        return jnp.sum(each_unit(one, jnp.zeros((8, LANES), I32)), axis=0, keepdims=True)

    @pl.when(n * BLOCK < TOPK_MAX)
    def _():
        write_sel(lambda k, kp: k == k)

    @pl.when(n * BLOCK >= TOPK_MAX)
    def _():
        def copy_first(u, carry):
            act_ref[pack_unit(u), :] = dig_ref[0, pack_unit(u), :]
            return carry

        each_unit(copy_first, 0)
        need = jnp.full((1, LANES), TOPK_MAX, I32)
        ukey_thr = jnp.zeros((1, LANES), I32)
        for p, nbits in enumerate(DIGIT_BITS):
            d, above = digit_search(nbits, need)
            ukey_thr = (ukey_thr << nbits) | d
            need = need - above
            if p + 1 < len(DIGIT_BITS):
                narrow(p, d, first=(p == 0))
        thr = ukey_thr ^ INT_MIN
        nge = write_sel(lambda k, kp: k >= thr)

        @pl.when(jnp.max(nge) > TOPK_MAX)
        def _():
            missing = need
            pos_bits = (ki_ref.shape[1] - 1).bit_length()

            def tie_positions(u, carry):
                for j in range(UNIT // SUB):
                    rows = sub_rows(u, j)
                    tie = key_ref[rows, :] == thr
                    sel_ref[rows, :] = jnp.where(tie, key_pos(u, j), INT_MAX).astype(F32)
                return carry

            each_unit(tie_positions, 0)

            def count_below(cj):
                cf = cj.astype(F32)

                def one(u, acc):
                    for j in range(UNIT // SUB):
                        m = (sel_ref[sub_rows(u, j), :] < cf).astype(I32)
                        acc = acc + jnp.sum(m.reshape(SUB // 8, 8, LANES), axis=0)
                    return acc

                return jnp.sum(each_unit(one, jnp.zeros((8, LANES), I32)), axis=0, keepdims=True)

            def jbit(i, jcut):
                cj = jcut | (jnp.int32(1) << (pos_bits - 1 - i))
                return jnp.where(count_below(cj) < missing, cj, jcut)

            jcut = lax.fori_loop(0, pos_bits, jbit, jnp.zeros((1, LANES), I32))
            jcut = jnp.where(nge > TOPK_MAX, jcut, INT_MAX)
            write_sel(lambda k, kp: (k > thr) | ((k == thr) & (kp <= jcut)))

    width = DSA_Q_HEADS * LANES
    oacc_ref[...] = jnp.zeros(oacc_ref.shape, F32)

    def scores(u, j):
        rows = sub_rows(u, j)
        sel = sel_ref[rows, :] > 0.0
        s = jnp.dot(kb_ref[0, rows, :], qs_ref[...], preferred_element_type=F32)
        tops = []
        for h in range(DSA_Q_HEADS):
            sh = jnp.where(sel, head(s, h), -jnp.inf)
            s_ref[j, :, h * LANES:(h + 1) * LANES] = sh
            tops.append(jnp.max(sh.reshape(SUB // 8, 8, LANES), axis=0))
        return jnp.max(jnp.concatenate(tops, axis=1), axis=0, keepdims=True)

    def accumulate(u, j, m_run, top):
        m_new = jnp.maximum(m_run, top)
        p = jnp.exp2(s_ref[j] - m_new)
        oacc_ref[...] = oacc_ref[...] * jnp.exp2(m_run - m_new) + jnp.dot(
            vt_ref[u * (UNIT // SUB) + j], p.astype(BF16), preferred_element_type=F32)
        return m_new

    def attn_unit(u, carry):
        m_run, top0 = carry
        top1 = scores(u, 1)
        m_run = accumulate(u, 0, m_run, top0)
        top0 = scores(jnp.minimum(u + 1, nu - 1), 0)
        m_run = accumulate(u, 1, m_run, top1)
        return m_run, top0

    lax.fori_loop(0, nu, attn_unit, (jnp.full((1, width), NEG, F32), scores(0, 0)))
    o = oacc_ref[...]
    o = o * (1.0 / o[HEAD_DIM:HEAD_DIM + 1, :])
    row = lax.broadcasted_iota(jnp.int32, (LANES, BLOCK), 0)
    for j in range(DSA_Q_HEADS // 2):
        pair = jnp.where(row < HEAD_DIM, head(o, 2 * j), pltpu.roll(head(o, 2 * j + 1), HEAD_DIM, 0))
        o_ref[0, :, j * LANES:(j + 1) * LANES] = pair.T.astype(o_ref.dtype)


def _dsa(qi, wi, qb, ki, kb, vt):
    b, s, _ = qb.shape
    cur = lambda w: pl.BlockSpec((1, BLOCK, w), lambda bi, n: (bi, n, 0))
    seq = pl.BlockSpec((1, s, LANES), lambda bi, n: (bi, 0, 0))
    rows = DSA_Q_HEADS * BLOCK
    return pl.pallas_call(
        _dsa_kernel,
        grid=(b, s // BLOCK),
        in_specs=[cur(QI_W), cur(LANES), cur(QB_W), seq, seq,
                  pl.BlockSpec((s // SUB, LANES, SUB), lambda bi, n: (bi, 0, 0))],
        out_specs=cur(QB_W),
        out_shape=jax.ShapeDtypeStruct((b, s, QB_W), BF16),
        scratch_shapes=[
            pltpu.VMEM((LANES, IDX_HEADS * BLOCK), BF16),
            pltpu.VMEM((LANES, rows), BF16),
            pltpu.VMEM((s, LANES), I32),
            pltpu.VMEM((len(DIGIT_BITS), s // FIELDS, LANES), I32),
            pltpu.VMEM((s // FIELDS, LANES), I32),
            pltpu.VMEM((s // FIELDS, LANES), I32),
            pltpu.VMEM((s, LANES), F32),
            pltpu.VMEM((UNIT // SUB, SUB, rows), F32),
            pltpu.VMEM((LANES, rows), F32),
        ],
        compiler_params=pltpu.CompilerParams(vmem_limit_bytes=VMEM_LIMIT),
        name="dsa_attn",
    )(qi, wi, qb, ki, kb, vt)


def _ffn_kernel(alpha, d_ff, oa_ref, ob_ref, x_ref, p_ref, wo_ref, g1_ref, b1_ref,
                wgu_ref, wd_ref, wpg_ref, wpp_ref, g_ref, b_ref, y_ref, yb_ref, h_ref):
    mix = (jnp.dot(oa_ref[...], wo_ref[:QA_W, :], preferred_element_type=F32)
           + jnp.dot(ob_ref[...], wo_ref[QA_W:, :], preferred_element_type=F32))
    x1 = _ln(alpha * x_ref[...] + mix, g1_ref[...], b1_ref[...])
    xb = x1.astype(BF16)
    fc = 2 * LANES
    for j in range(d_ff // fc):
        g = jnp.dot(xb, wgu_ref[:, j * fc:(j + 1) * fc], preferred_element_type=F32)
        u = jnp.dot(xb, wgu_ref[:, d_ff + j * fc:d_ff + (j + 1) * fc], preferred_element_type=F32)
        h_ref[:, j * fc:(j + 1) * fc] = (g * jax.nn.sigmoid(g) * u).astype(BF16)
    ffn = jnp.dot(h_ref[...], wd_ref[...], preferred_element_type=F32)
    gate = jax.nn.sigmoid(jnp.dot(xb, wpg_ref[...], preferred_element_type=F32))
    ple = jnp.dot(p_ref[...].astype(BF16), wpp_ref[...], preferred_element_type=F32) * gate
    y = _ln(alpha * x1 + ffn + ple, g_ref[...], b_ref[...])
    y_ref[...] = y
    yb_ref[...] = y.astype(BF16)


def _ffn(oa, ob, x, p, wo, g1, b1, wgu, wd, wpg, wpp, g, b, alpha):
    t = x.shape[0]
    d_ff = wd.shape[0]
    tm = 512
    row = lambda w: pl.BlockSpec((tm, w), lambda i: (i, 0))
    const = lambda shape: pl.BlockSpec(shape, lambda i: (0, 0), pipeline_mode=pl.Buffered(1))
    vec = const((1, D_MODEL))
    return pl.pallas_call(
        functools.partial(_ffn_kernel, alpha, d_ff),
        grid=(t // tm,),
        in_specs=[row(QA_W), row(QB_W), row(D_MODEL), row(PLE_DIM), const(wo.shape), vec, vec,
                  const(wgu.shape), const(wd.shape), const(wpg.shape), const(wpp.shape), vec, vec],
        out_specs=[row(D_MODEL), row(D_MODEL)],
        out_shape=[jax.ShapeDtypeStruct((t, D_MODEL), F32), jax.ShapeDtypeStruct((t, D_MODEL), BF16)],
        scratch_shapes=[pltpu.VMEM((tm, d_ff), BF16)],
        compiler_params=pltpu.CompilerParams(vmem_limit_bytes=VMEM_LIMIT),
        name="mix_ffn_ln",
    )(oa, ob, x, p, wo, g1, b1, wgu, wd, wpg, wpp, g, b)


def _extend_w_in(w):
    cuts = [0]
    for sz in IN_SIZES:
        cuts.append(cuts[-1] + sz)
    qa, ka, va, qb, kb, vb, qi, ki, wi = [w[:, cuts[i]:cuts[i + 1]] for i in range(len(IN_SIZES))]
    pad = jnp.zeros((w.shape[0], LANES - IDX_HEADS), w.dtype)
    ext = jnp.concatenate([qa, ka, va, qb, kb, kb, vb, vb, qi, ki, ki, ki, ki, wi, pad], axis=1)
    return ext.astype(BF16)


def kernel(x, p, positions, ln_in_g, ln_in_b, w_in, attn_sinks, idx_k_g, idx_k_b, w_o, ln1_g, ln1_b,
           w_gu, w_down, w_pg, w_pp, ln2_g, ln2_b):
    bsz, seq, d = x.shape
    depth = w_in.shape[0]
    t = bsz * seq
    alpha = (2 * depth) ** 0.25
    row = lambda v: v.reshape(1, -1).astype(F32)

    inv64 = ROPE_THETA ** (-jnp.arange(HEAD_DIM // 2, dtype=F32) / (HEAD_DIM // 2))
    inv32 = ROPE_THETA ** (-jnp.arange(IDX_DIM // 2, dtype=F32) / (IDX_DIM // 2))
    inv_row = jnp.concatenate([inv64, inv32, jnp.zeros((LANES - 48,), F32)]).reshape(1, LANES)
    tables = _rope_tables(positions.reshape(t, 1), inv_row)

    xf = x.reshape(t, d)
    xb = None
    for i in range(depth):
        w_ext = _extend_w_in(w_in[i])
        kig = jnp.tile(idx_k_g[i], LANES // IDX_DIM).reshape(1, LANES)
        kib = jnp.tile(idx_k_b[i], LANES // IDX_DIM).reshape(1, LANES)
        if i == 0:
            xf, *segs = _in_proj(xf, w_ext, tables, kig, kib, ln=(row(ln_in_g), row(ln_in_b)))
        else:
            segs = _in_proj(xb, w_ext, tables, kig, kib)
        qa, ka, va, qb, kb, vt, qi, ki, wi = [a if a.ndim == 3 else a.reshape(bsz, seq, -1) for a in segs]
        sink_row = jnp.repeat(attn_sinks[i].astype(F32), BLOCK).reshape(1, -1)
        oa = _swa(qa, ka, va, sink_row).reshape(t, QA_W)
        ob = _dsa(qi, wi, qb, ki, kb, vt).reshape(t, QB_W)
        xf, xb = _ffn(oa, ob, xf, p[i].reshape(t, PLE_DIM), w_o[i].astype(BF16), row(ln1_g[i]), row(ln1_b[i]),
                      w_gu[i].astype(BF16), w_down[i].astype(BF16), w_pg[i].astype(BF16), w_pp[i].astype(BF16),
                      row(ln2_g[i]), row(ln2_b[i]), alpha)
    return xf.reshape(bsz, seq, d)
```

```python
import functools

import jax
import jax.numpy as jnp
from jax import lax
from jax.experimental import pallas as pl
from jax.experimental.pallas import tpu as pltpu

D_MODEL = 1024
HEAD_DIM = 64
SWA_Q_HEADS = 8
SWA_KV_HEADS = 2
DSA_Q_HEADS = 8
IDX_HEADS = 8
IDX_DIM = 32
TOPK_MAX = 256
WINDOW = 128
BLOCK = 128
ROPE_THETA = 10000.0
PLE_DIM = 256
LN_EPS = 1e-5
NEG = -1e30
LOG2E = 1.4426950408889634
LANES = 128
QA_W = SWA_Q_HEADS * HEAD_DIM
QB_W = DSA_Q_HEADS * HEAD_DIM
QI_W = IDX_HEADS * IDX_DIM
IN_SIZES = (QA_W, SWA_KV_HEADS * HEAD_DIM, SWA_KV_HEADS * HEAD_DIM, QB_W, HEAD_DIM, HEAD_DIM,
            QI_W, IDX_DIM, IDX_HEADS)
EXT_SEGS = (("qa", QA_W), ("ka", LANES), ("va", LANES), ("qb", QB_W), ("kb", LANES),
            ("vt", LANES), ("qi", QI_W), ("ki", LANES), ("wi", LANES))
EXT_W = sum(w for _, w in EXT_SEGS)
SWA_STEP = 8
SUB = 256
UNIT = 2 * SUB
INT_MIN = -2 ** 31
INT_MAX = 2 ** 31 - 1
VMEM_LIMIT = 56 * 1024 * 1024
FIELDS = 4
FIELD_BITS = 32 // FIELDS
DIGIT_BITS = (7, 7, 7, 7, 4)
GUARDS = sum(1 << (FIELD_BITS * f + FIELD_BITS - 1) for f in range(FIELDS)) - 2 ** 32
ONES = sum(1 << (FIELD_BITS * f) for f in range(FIELDS))
LOW7 = ONES * 0x7F
PACK = SUB // FIELDS

F32 = jnp.float32
BF16 = jnp.bfloat16
I32 = jnp.int32
srl = lax.shift_right_logical


def _ln(x, g, b):
    mu = jnp.mean(x, axis=-1, keepdims=True)
    xc = x - mu
    var = jnp.mean(xc * xc, axis=-1, keepdims=True)
    return xc * lax.rsqrt(var + LN_EPS) * g + b


def _rope_table_kernel(pos_ref, inv_ref, cos64_ref, sin64_ref, cos32_ref, sin32_ref):
    ang = pos_ref[...].astype(F32) * inv_ref[...]
    lane = lax.broadcasted_iota(jnp.int32, ang.shape, 1)
    c, s = jnp.cos(ang), jnp.sin(ang)

    def tile64(a):
        a = jnp.where(lane < 32, a, pltpu.roll(a, 32, 1))
        return jnp.where(lane < 64, a, pltpu.roll(a, 64, 1))

    def tile32(a):
        a = pltpu.roll(a, 96, 1)
        a = jnp.where(lane < 16, a, pltpu.roll(a, 16, 1))
        a = jnp.where(lane < 32, a, pltpu.roll(a, 32, 1))
        return jnp.where(lane < 64, a, pltpu.roll(a, 64, 1))

    cos64_ref[...] = tile64(c)
    sin64_ref[...] = jnp.where(lane % 64 < 32, -1.0, 1.0) * tile64(s)
    cos32_ref[...] = tile32(c)
    sin32_ref[...] = jnp.where(lane % 32 < 16, -1.0, 1.0) * tile32(s)


def _rope_tables(pos_col, inv_row):
    t = pos_col.shape[0]
    tm = 1024
    row = pl.BlockSpec((tm, LANES), lambda i: (i, 0))
    return pl.pallas_call(
        _rope_table_kernel,
        grid=(t // tm,),
        in_specs=[pl.BlockSpec((tm, 1), lambda i: (i, 0)),
                  pl.BlockSpec((1, LANES), lambda i: (0, 0))],
        out_specs=[row] * 4,
        out_shape=[jax.ShapeDtypeStruct((t, LANES), F32)] * 4,
        name="rope_tables",
    )(pos_col, inv_row)


def _rope_apply(h, cos, sin_signed, half):
    lane = lax.broadcasted_iota(jnp.int32, h.shape, 1)
    partner = jnp.where(lane % (2 * half) < half,
                        pltpu.roll(h, LANES - half, 1), pltpu.roll(h, half, 1))
    return h * cos + partner * sin_signed


def _in_proj_kernel(apply_ln, *refs):
    if apply_ln:
        (x_ref, lng_ref, lnb_ref, w_ref, cos64_ref, sin64_ref, cos32_ref, sin32_ref,
         kig_ref, kib_ref, xn_ref, *outs) = refs
        xn = _ln(x_ref[...], lng_ref[...], lnb_ref[...])
        xn_ref[...] = xn
        xb = xn.astype(BF16)
    else:
        (x_ref, w_ref, cos64_ref, sin64_ref, cos32_ref, sin32_ref,
         kig_ref, kib_ref, *outs) = refs
        xb = x_ref[...]
    out = dict(zip([n for n, _ in EXT_SEGS], outs))
    cos64, sin64 = cos64_ref[...], sin64_ref[...]
    cos32, sin32 = cos32_ref[...], sin32_ref[...]
    chunks = [(name, j) for name, width in EXT_SEGS for j in range(width // LANES)]
    mxu_n = 2 * LANES
    for c0 in range(0, len(chunks), mxu_n // LANES):
        group = chunks[c0:c0 + mxu_n // LANES]
        hh = jnp.dot(xb, w_ref[:, c0 * LANES:(c0 + len(group)) * LANES], preferred_element_type=F32)
        for k, (name, j) in enumerate(group):
            h = hh[:, k * LANES:(k + 1) * LANES]
            if name in ("qa", "qb"):
                h = _rope_apply(h, cos64, sin64, HEAD_DIM // 2) * (HEAD_DIM ** -0.5 * LOG2E)
            elif name in ("ka", "kb"):
                h = _rope_apply(h, cos64, sin64, HEAD_DIM // 2)
            elif name == "qi":
                h = _rope_apply(h, cos32, sin32, IDX_DIM // 2)
            elif name == "ki":
                h = _rope_apply(_ln(h, kig_ref[...], kib_ref[...]), cos32, sin32, IDX_DIM // 2)
            elif name == "wi":
                h = h * (IDX_HEADS ** -0.5 * IDX_DIM ** -0.5)
            if name == "vt":
                h = jnp.where(lax.broadcasted_iota(jnp.int32, h.shape, 1) < HEAD_DIM, h, 1.0)
                for r in range(h.shape[0] // SUB):
                    out[name][r] = h[r * SUB:(r + 1) * SUB].T.astype(BF16)
            else:
                out[name][:, j * LANES:(j + 1) * LANES] = h.astype(out[name].dtype)


def _in_proj(x, w_ext, tables, kig, kib, ln=None):
    t = x.shape[0]
    tm = 1024
    row = lambda w: pl.BlockSpec((tm, w), lambda i: (i, 0))
    const = lambda shape: pl.BlockSpec(shape, lambda i: (0, 0))
    in_specs = [row(D_MODEL)]
    args = [x]
    if ln is not None:
        in_specs += [const((1, D_MODEL))] * 2
        args += list(ln)
    in_specs += [const((D_MODEL, EXT_W))] + [row(LANES)] * 4 + [const((1, LANES))] * 2
    args += [w_ext, *tables, kig, kib]
    out_specs, out_shape = [], []
    if ln is not None:
        out_specs.append(row(D_MODEL))
        out_shape.append(jax.ShapeDtypeStruct((t, D_MODEL), F32))
    for name, width in EXT_SEGS:
        if name == "vt":
            out_specs.append(pl.BlockSpec((tm // SUB, LANES, SUB), lambda i: (i, 0, 0)))
            out_shape.append(jax.ShapeDtypeStruct((t // SUB, LANES, SUB), BF16))
            continue
        out_specs.append(row(width))
        out_shape.append(jax.ShapeDtypeStruct((t, width), F32 if name == "wi" else BF16))
    return pl.pallas_call(
        functools.partial(_in_proj_kernel, ln is not None),
        grid=(t // tm,),
        in_specs=in_specs, out_specs=out_specs, out_shape=out_shape,
        compiler_params=pltpu.CompilerParams(vmem_limit_bytes=VMEM_LIMIT),
        name="in_proj",
    )(*args)


def _swa_kernel(q_ref, kc_ref, kp_ref, vc_ref, vp_ref, sink_ref, o_ref):
    n = pl.program_id(1)
    row = lax.broadcasted_iota(jnp.int32, (LANES, BLOCK), 0)
    head = lambda a, h: a[:, h * LANES:(h + 1) * LANES]
    group = SWA_Q_HEADS // SWA_KV_HEADS
    swap = lambda a: pltpu.roll(a, HEAD_DIM, 0)
    kidx = lax.broadcasted_iota(jnp.int32, (2 * BLOCK, BLOCK), 0)
    diff = BLOCK + lax.broadcasted_iota(jnp.int32, (2 * BLOCK, BLOCK), 1) - kidx
    band = (diff >= 0) & (diff < WINDOW)
    sink = sink_ref[...] * LOG2E
    for i in range(SWA_STEP):
        rows = slice(i * BLOCK, (i + 1) * BLOCK)
        q = q_ref[0, rows, :].astype(F32)
        cols = []
        for j in range(SWA_Q_HEADS // 2):
            qt = head(q, j).T
            kv = (2 * j) // group
            keep = (row < HEAD_DIM) if kv == 0 else (row >= HEAD_DIM)
            even, odd = (qt, swap(qt)) if kv == 0 else (swap(qt), qt)
            cols += [jnp.where(keep, even, 0.0), jnp.where(keep, odd, 0.0)]
        qt = jnp.concatenate(cols, axis=1).astype(BF16)
        if i == 0:
            kk = jnp.concatenate([kp_ref[0], kc_ref[0, rows, :]], 0)
            vv = jnp.concatenate([vp_ref[0], vc_ref[0, rows, :]], 0)
            first_key = jnp.where(n > 0, 0, BLOCK)
            mask = band & (kidx >= first_key)
        else:
            kk = kc_ref[0, (i - 1) * BLOCK:(i + 1) * BLOCK, :]
            vv = vc_ref[0, (i - 1) * BLOCK:(i + 1) * BLOCK, :]
            mask = band
        s = jnp.dot(kk, qt, preferred_element_type=F32)
        s = jnp.concatenate([jnp.where(mask, head(s, h), NEG) for h in range(SWA_Q_HEADS)], axis=1)
        m = jnp.maximum(jnp.max(s, axis=0, keepdims=True), sink)
        p = jnp.exp2(s - m)
        r = 1.0 / (jnp.sum(p, axis=0, keepdims=True) + jnp.exp2(sink - m))
        vt = vv.astype(F32).T.astype(BF16)
        ot = jnp.dot(vt, p.astype(BF16), preferred_element_type=F32) * r
        for j in range(SWA_Q_HEADS // 2):
            a, b = head(ot, 2 * j), head(ot, 2 * j + 1)
            pair = jnp.where(row < HEAD_DIM, a, swap(b)) if (2 * j) // group == 0 else \
                jnp.where(row < HEAD_DIM, swap(a), b)
            o_ref[0, rows, j * LANES:(j + 1) * LANES] = pair.T.astype(o_ref.dtype)


def _swa(qa, ka, va, sink_row):
    b, s, _ = qa.shape
    tq = SWA_STEP * BLOCK
    cur = lambda w: pl.BlockSpec((1, tq, w), lambda bi, n: (bi, n, 0))
    prev = lambda w: pl.BlockSpec((1, BLOCK, w), lambda bi, n: (bi, jnp.maximum(n * SWA_STEP - 1, 0), 0))
    return pl.pallas_call(
        _swa_kernel,
        grid=(b, s // tq),
        in_specs=[cur(QA_W), cur(LANES), prev(LANES), cur(LANES), prev(LANES),
                  pl.BlockSpec((1, SWA_Q_HEADS * LANES), lambda bi, n: (0, 0))],
        out_specs=cur(QA_W),
        out_shape=jax.ShapeDtypeStruct((b, s, QA_W), BF16),
        name="swa_attn",
    )(qa, ka, ka, va, va, sink_row)


def _tree_sum(parts):
    while len(parts) > 1:
        parts = [a + b for a, b in zip(parts[::2], parts[1::2])] + ([parts[-1]] if len(parts) % 2 else [])
    return parts[0]


def _pack_digits(ukey, shift, bits):
    word = None
    for f in range(FIELDS):
        k = ukey[f * PACK:(f + 1) * PACK]
        pos = FIELD_BITS * f
        k = srl(k, shift - pos) if shift >= pos else k << (pos - shift)
        k = k & (((1 << bits) - 1) << pos)
        word = k if word is None else word | k
    return word | GUARDS


def _field_sum(acc):
    tot = _tree_sum([srl(acc, FIELD_BITS * f) & 0xFF for f in range(FIELDS)])
    return jnp.sum(tot, axis=0, keepdims=True)


def _dsa_kernel(qi_ref, wi_ref, qb_ref, ki_ref, kb_ref, vt_ref, o_ref,
                qim_ref, qs_ref, key_ref, dig_ref, act_ref, live_ref, sel_ref, s_ref, oacc_ref):
    n = pl.program_id(1)
    nu = n // (UNIT // BLOCK) + 1
    lane = lax.broadcasted_iota(jnp.int32, (BLOCK, LANES), 1)
    sub_rows = lambda u, j: pl.ds(pl.multiple_of(u * UNIT + j * SUB, SUB), SUB)
    pack_rows = lambda u, j: pl.ds(pl.multiple_of((u * UNIT + j * SUB) // FIELDS, PACK), PACK)
    pack_unit = lambda u: pl.ds(pl.multiple_of(u * (UNIT // FIELDS), UNIT // FIELDS), UNIT // FIELDS)
    key_pos = lambda u, j: u * UNIT + j * SUB + lax.broadcasted_iota(jnp.int32, (SUB, LANES), 0)
    causal = lambda u, j: key_pos(u, j) <= n * BLOCK + lax.broadcasted_iota(jnp.int32, (SUB, LANES), 1)
    head = lambda a, h: a[:, h * LANES:(h + 1) * LANES]
    words = lambda a: a.reshape(a.shape[0] // 8, 8, LANES)

    def each_unit(one, init):
        carry = lax.fori_loop(0, nu // 2, lambda i, c: one(2 * i + 1, one(2 * i, c)), init)
        return lax.cond(nu % 2 == 1, lambda: one(nu - 1, carry), lambda: carry)

    qi = qi_ref[0].astype(F32)
    for h in range(IDX_HEADS):
        per = LANES // IDX_DIM
        qim_ref[:, h * BLOCK:(h + 1) * BLOCK] = jnp.where(
            lane // IDX_DIM == h % per, head(qi, h // per), 0.0).T.astype(BF16)
    qb = qb_ref[0].astype(F32)
    for h in range(DSA_Q_HEADS):
        keep = (lane < HEAD_DIM) if h % 2 == 0 else (lane >= HEAD_DIM)
        qs_ref[:, h * BLOCK:(h + 1) * BLOCK] = jnp.where(keep, head(qb, h // 2), 0.0).T.astype(BF16)
    wt = wi_ref[0].T
    w_rows = [wt[h:h + 1, :] for h in range(IDX_HEADS)]

    shifts = [sum(DIGIT_BITS[i + 1:]) for i in range(len(DIGIT_BITS))]

    def idx_unit(u, carry):
        for j in range(UNIT // SUB):
            rows = sub_rows(u, j)
            sc = jnp.dot(ki_ref[0, rows, :], qim_ref[...], preferred_element_type=F32)
            acc = jnp.maximum(head(sc, 0), 0.0) * w_rows[0]
            for h in range(1, IDX_HEADS):
                acc = acc + jnp.maximum(head(sc, h), 0.0) * w_rows[h]
            acc = jnp.where(acc == 0.0, 0.0, acc)
            idx = jnp.where(causal(u, j), acc, NEG)
            bits = lax.bitcast_convert_type(idx, I32)
            key = bits ^ ((bits >> 31) & INT_MAX)
            key_ref[rows, :] = key
            ukey = key ^ INT_MIN
            for p, (shift, nbits) in enumerate(zip(shifts, DIGIT_BITS)):
                dig_ref[p, pack_rows(u, j), :] = _pack_digits(ukey, shift, nbits)
        return carry

    lax.fori_loop(0, nu, idx_unit, 0)

    def count_ge(cand):
        crep = jnp.broadcast_to(cand * ONES, (8, LANES))

        def one(u, acc):
            a = words(act_ref[pack_unit(u), :])
            t = srl(a - crep[None], FIELD_BITS - 1) & ONES
            return acc + _tree_sum([t[i] for i in range(t.shape[0])])

        return _field_sum(each_unit(one, jnp.zeros((8, LANES), I32)))

    def digit_search(nbits, need):
        def bit(i, carry):
            d, above = carry
            cand = d | (jnp.int32(1) << (nbits - 1 - i))
            cnt = count_ge(cand)
            ok = cnt >= need
            return jnp.where(ok, cand, d), jnp.where(ok, above, cnt)

        return lax.fori_loop(0, nbits, bit, (jnp.zeros((1, LANES), I32), jnp.zeros((1, LANES), I32)))

    def narrow(p, digit, first):
        want = jnp.broadcast_to((digit * ONES) | GUARDS, (8, LANES))[None]

        def one(u, carry):
            rows = pack_unit(u)
            e = words(dig_ref[p, rows, :]) ^ want
            nonzero = (((e & LOW7) + LOW7) | e) & GUARDS
            zero = nonzero ^ GUARDS
            match = zero | (zero - srl(zero, FIELD_BITS - 1))
            if not first:
                match = match & words(live_ref[rows, :])
            live_ref[rows, :] = match.reshape(UNIT // FIELDS, LANES)
            nxt = (words(dig_ref[p + 1, rows, :]) & match) | GUARDS
            act_ref[rows, :] = nxt.reshape(UNIT // FIELDS, LANES)
            return carry

        each_unit(one, 0)

    def write_sel(keep):
        def one(u, acc):
            for j in range(UNIT // SUB):
                rows = sub_rows(u, j)
                m = keep(key_ref[rows, :], key_pos(u, j))
                sel_ref[rows, :] = (m & causal(u, j)).astype(F32)
                acc = acc + jnp.sum(m.astype(I32).reshape(SUB // 8, 8, LANES), axis=0)
            return acc

        return jnp.sum(each_unit(one, jnp.zeros((8, LANES), I32)), axis=0, keepdims=True)

    @pl.when(n * BLOCK < TOPK_MAX)
    def _():
        write_sel(lambda k, kp: k == k)

    @pl.when(n * BLOCK >= TOPK_MAX)
    def _():
        def copy_first(u, carry):
            act_ref[pack_unit(u), :] = dig_ref[0, pack_unit(u), :]
            return carry

        each_unit(copy_first, 0)
        need = jnp.full((1, LANES), TOPK_MAX, I32)
        ukey_thr = jnp.zeros((1, LANES), I32)
        for p, nbits in enumerate(DIGIT_BITS):
            d, above = digit_search(nbits, need)
            ukey_thr = (ukey_thr << nbits) | d
            need = need - above
            if p + 1 < len(DIGIT_BITS):
                narrow(p, d, first=(p == 0))
        thr = ukey_thr ^ INT_MIN
        nge = write_sel(lambda k, kp: k >= thr)

        @pl.when(jnp.max(nge) > TOPK_MAX)
        def _():
            missing = need
            pos_bits = (ki_ref.shape[1] - 1).bit_length()

            def tie_positions(u, carry):
                for j in range(UNIT // SUB):
                    rows = sub_rows(u, j)
                    tie = key_ref[rows, :] == thr
                    sel_ref[rows, :] = jnp.where(tie, key_pos(u, j), INT_MAX).astype(F32)
                return carry

            each_unit(tie_positions, 0)

            def count_below(cj):
                cf = cj.astype(F32)

                def one(u, acc):
                    for j in range(UNIT // SUB):
                        m = (sel_ref[sub_rows(u, j), :] < cf).astype(I32)
                        acc = acc + jnp.sum(m.reshape(SUB // 8, 8, LANES), axis=0)
                    return acc

                return jnp.sum(each_unit(one, jnp.zeros((8, LANES), I32)), axis=0, keepdims=True)

            def jbit(i, jcut):
                cj = jcut | (jnp.int32(1) << (pos_bits - 1 - i))
                return jnp.where(count_below(cj) < missing, cj, jcut)

            jcut = lax.fori_loop(0, pos_bits, jbit, jnp.zeros((1, LANES), I32))
            jcut = jnp.where(nge > TOPK_MAX, jcut, INT_MAX)
            write_sel(lambda k, kp: (k > thr) | ((k == thr) & (kp <= jcut)))

    width = DSA_Q_HEADS * LANES
    oacc_ref[...] = jnp.zeros(oacc_ref.shape, F32)

    def scores(u, j):
        rows = sub_rows(u, j)
        sel = sel_ref[rows, :] > 0.0
        s = jnp.dot(kb_ref[0, rows, :], qs_ref[...], preferred_element_type=F32)
        tops = []
        for h in range(DSA_Q_HEADS):
            sh = jnp.where(sel, head(s, h), -jnp.inf)
            s_ref[j, :, h * LANES:(h + 1) * LANES] = sh
            tops.append(jnp.max(sh.reshape(SUB // 8, 8, LANES), axis=0))
        return jnp.max(jnp.concatenate(tops, axis=1), axis=0, keepdims=True)

    def accumulate(u, j, m_run, top):
        m_new = jnp.maximum(m_run, top)
        p = jnp.exp2(s_ref[j] - m_new)
        oacc_ref[...] = oacc_ref[...] * jnp.exp2(m_run - m_new) + jnp.dot(
            vt_ref[u * (UNIT // SUB) + j], p.astype(BF16), preferred_element_type=F32)
        return m_new

    def attn_unit(u, carry):
        m_run, top0 = carry
        top1 = scores(u, 1)
        m_run = accumulate(u, 0, m_run, top0)
        top0 = scores(jnp.minimum(u + 1, nu - 1), 0)
        m_run = accumulate(u, 1, m_run, top1)
        return m_run, top0

    lax.fori_loop(0, nu, attn_unit, (jnp.full((1, width), NEG, F32), scores(0, 0)))
    o = oacc_ref[...]
    o = o * (1.0 / o[HEAD_DIM:HEAD_DIM + 1, :])
    row = lax.broadcasted_iota(jnp.int32, (LANES, BLOCK), 0)
    for j in range(DSA_Q_HEADS // 2):
        pair = jnp.where(row < HEAD_DIM, head(o, 2 * j), pltpu.roll(head(o, 2 * j + 1), HEAD_DIM, 0))
        o_ref[0, :, j * LANES:(j + 1) * LANES] = pair.T.astype(o_ref.dtype)


def _dsa(qi, wi, qb, ki, kb, vt):
    b, s, _ = qb.shape
    assert s % UNIT == 0 and s // 4 >= TOPK_MAX, "sequence must be a multiple of UNIT with k_top == TOPK_MAX"
    assert s // FIELDS // 8 < 2 ** FIELD_BITS, "per-field counters of the threshold search would overflow"
    cur = lambda w: pl.BlockSpec((1, BLOCK, w), lambda bi, n: (bi, n, 0))
    seq = pl.BlockSpec((1, s, LANES), lambda bi, n: (bi, 0, 0))
    rows = DSA_Q_HEADS * BLOCK
    return pl.pallas_call(
        _dsa_kernel,
        grid=(b, s // BLOCK),
        in_specs=[cur(QI_W), cur(LANES), cur(QB_W), seq, seq,
                  pl.BlockSpec((s // SUB, LANES, SUB), lambda bi, n: (bi, 0, 0))],
        out_specs=cur(QB_W),
        out_shape=jax.ShapeDtypeStruct((b, s, QB_W), BF16),
        scratch_shapes=[
            pltpu.VMEM((LANES, IDX_HEADS * BLOCK), BF16),
            pltpu.VMEM((LANES, rows), BF16),
            pltpu.VMEM((s, LANES), I32),
            pltpu.VMEM((len(DIGIT_BITS), s // FIELDS, LANES), I32),
            pltpu.VMEM((s // FIELDS, LANES), I32),
            pltpu.VMEM((s // FIELDS, LANES), I32),
            pltpu.VMEM((s, LANES), F32),
            pltpu.VMEM((UNIT // SUB, SUB, rows), F32),
            pltpu.VMEM((LANES, rows), F32),
        ],
        compiler_params=pltpu.CompilerParams(vmem_limit_bytes=VMEM_LIMIT),
        name="dsa_attn",
    )(qi, wi, qb, ki, kb, vt)


def _ffn_kernel(alpha, d_ff, oa_ref, ob_ref, x_ref, p_ref, wo_ref, g1_ref, b1_ref,
                wgu_ref, wd_ref, wpg_ref, wpp_ref, g_ref, b_ref, y_ref, yb_ref, h_ref):
    mix = (jnp.dot(oa_ref[...], wo_ref[:QA_W, :], preferred_element_type=F32)
           + jnp.dot(ob_ref[...], wo_ref[QA_W:, :], preferred_element_type=F32))
    x1 = _ln(alpha * x_ref[...] + mix, g1_ref[...], b1_ref[...])
    xb = x1.astype(BF16)
    fc = 2 * LANES
    for j in range(d_ff // fc):
        g = jnp.dot(xb, wgu_ref[:, j * fc:(j + 1) * fc], preferred_element_type=F32)
        u = jnp.dot(xb, wgu_ref[:, d_ff + j * fc:d_ff + (j + 1) * fc], preferred_element_type=F32)
        h_ref[:, j * fc:(j + 1) * fc] = (g * jax.nn.sigmoid(g) * u).astype(BF16)
    ffn = jnp.dot(h_ref[...], wd_ref[...], preferred_element_type=F32)
    gate = jax.nn.sigmoid(jnp.dot(xb, wpg_ref[...], preferred_element_type=F32))
    ple = jnp.dot(p_ref[...].astype(BF16), wpp_ref[...], preferred_element_type=F32) * gate
    y = _ln(alpha * x1 + ffn + ple, g_ref[...], b_ref[...])
    y_ref[...] = y
    yb_ref[...] = y.astype(BF16)


def _ffn(oa, ob, x, p, wo, g1, b1, wgu, wd, wpg, wpp, g, b, alpha):
    t = x.shape[0]
    d_ff = wd.shape[0]
    tm = 512
    row = lambda w: pl.BlockSpec((tm, w), lambda i: (i, 0))
    const = lambda shape: pl.BlockSpec(shape, lambda i: (0, 0), pipeline_mode=pl.Buffered(1))
    vec = const((1, D_MODEL))
    return pl.pallas_call(
        functools.partial(_ffn_kernel, alpha, d_ff),
        grid=(t // tm,),
        in_specs=[row(QA_W), row(QB_W), row(D_MODEL), row(PLE_DIM), const(wo.shape), vec, vec,
                  const(wgu.shape), const(wd.shape), const(wpg.shape), const(wpp.shape), vec, vec],
        out_specs=[row(D_MODEL), row(D_MODEL)],
        out_shape=[jax.ShapeDtypeStruct((t, D_MODEL), F32), jax.ShapeDtypeStruct((t, D_MODEL), BF16)],
        scratch_shapes=[pltpu.VMEM((tm, d_ff), BF16)],
        compiler_params=pltpu.CompilerParams(vmem_limit_bytes=VMEM_LIMIT),
        name="mix_ffn_ln",
    )(oa, ob, x, p, wo, g1, b1, wgu, wd, wpg, wpp, g, b)


def _extend_w_in(w):
    cuts = [0]
    for sz in IN_SIZES:
        cuts.append(cuts[-1] + sz)
    qa, ka, va, qb, kb, vb, qi, ki, wi = [w[:, cuts[i]:cuts[i + 1]] for i in range(len(IN_SIZES))]
    pad = jnp.zeros((w.shape[0], LANES - IDX_HEADS), w.dtype)
    ext = jnp.concatenate([qa, ka, va, qb, kb, kb, vb, vb, qi, ki, ki, ki, ki, wi, pad], axis=1)
    return ext.astype(BF16)


def kernel(x, p, positions, ln_in_g, ln_in_b, w_in, attn_sinks, idx_k_g, idx_k_b, w_o, ln1_g, ln1_b,
           w_gu, w_down, w_pg, w_pp, ln2_g, ln2_b):
    bsz, seq, d = x.shape
    depth = w_in.shape[0]
    t = bsz * seq
    alpha = (2 * depth) ** 0.25
    row = lambda v: v.reshape(1, -1).astype(F32)

    inv64 = ROPE_THETA ** (-jnp.arange(HEAD_DIM // 2, dtype=F32) / (HEAD_DIM // 2))
    inv32 = ROPE_THETA ** (-jnp.arange(IDX_DIM // 2, dtype=F32) / (IDX_DIM // 2))
    inv_row = jnp.concatenate([inv64, inv32, jnp.zeros((LANES - 48,), F32)]).reshape(1, LANES)
    tables = _rope_tables(positions.reshape(t, 1), inv_row)

    xf = x.reshape(t, d)
    xb = None
    for i in range(depth):
        w_ext = _extend_w_in(w_in[i])
        kig = jnp.tile(idx_k_g[i], LANES // IDX_DIM).reshape(1, LANES)
        kib = jnp.tile(idx_k_b[i], LANES // IDX_DIM).reshape(1, LANES)
        if i == 0:
            xf, *segs = _in_proj(xf, w_ext, tables, kig, kib, ln=(row(ln_in_g), row(ln_in_b)))
        else:
            segs = _in_proj(xb, w_ext, tables, kig, kib)
        qa, ka, va, qb, kb, vt, qi, ki, wi = [a if a.ndim == 3 else a.reshape(bsz, seq, -1) for a in segs]
        sink_row = jnp.repeat(attn_sinks[i].astype(F32), BLOCK).reshape(1, -1)
        oa = _swa(qa, ka, va, sink_row).reshape(t, QA_W)
        ob = _dsa(qi, wi, qb, ki, kb, vt).reshape(t, QB_W)
        xf, xb = _ffn(oa, ob, xf, p[i].reshape(t, PLE_DIM), w_o[i].astype(BF16), row(ln1_g[i]), row(ln1_b[i]),
                      w_gu[i].astype(BF16), w_down[i].astype(BF16), w_pg[i].astype(BF16), w_pp[i].astype(BF16),
                      row(ln2_g[i]), row(ln2_b[i]), alpha)
    return xf.reshape(bsz, seq, d)
```

```python
import functools

import jax
import jax.numpy as jnp
from jax import lax
from jax.experimental import pallas as pl
from jax.experimental.pallas import tpu as pltpu

D_MODEL = 1024
HEAD_DIM = 64
SWA_Q_HEADS = 8
SWA_KV_HEADS = 2
DSA_Q_HEADS = 8
IDX_HEADS = 8
IDX_DIM = 32
TOPK_MAX = 256
WINDOW = 128
BLOCK = 128
ROPE_THETA = 10000.0
PLE_DIM = 256
LN_EPS = 1e-5
NEG = -1e30
LOG2E = 1.4426950408889634
LANES = 128
QA_W = SWA_Q_HEADS * HEAD_DIM
QB_W = DSA_Q_HEADS * HEAD_DIM
QI_W = IDX_HEADS * IDX_DIM
IN_SIZES = (QA_W, SWA_KV_HEADS * HEAD_DIM, SWA_KV_HEADS * HEAD_DIM, QB_W, HEAD_DIM, HEAD_DIM,
            QI_W, IDX_DIM, IDX_HEADS)
EXT_SEGS = (("qa", QA_W), ("ka", LANES), ("va", LANES), ("qb", QB_W), ("kb", LANES),
            ("vt", LANES), ("qi", QI_W), ("ki", LANES), ("wi", LANES))
EXT_W = sum(w for _, w in EXT_SEGS)
SWA_STEP = 8
SUB = 256
UNIT = 2 * SUB
INT_MIN = -2 ** 31
INT_MAX = 2 ** 31 - 1
VMEM_LIMIT = 56 * 1024 * 1024
FIELDS = 4
FIELD_BITS = 32 // FIELDS
DIGIT_BITS = (7, 7, 7, 7, 4)
GUARDS = sum(1 << (FIELD_BITS * f + FIELD_BITS - 1) for f in range(FIELDS)) - 2 ** 32
ONES = sum(1 << (FIELD_BITS * f) for f in range(FIELDS))
LOW7 = ONES * 0x7F
PACK = SUB // FIELDS

F32 = jnp.float32
BF16 = jnp.bfloat16
I32 = jnp.int32
srl = lax.shift_right_logical


def _ln(x, g, b):
    mu = jnp.mean(x, axis=-1, keepdims=True)
    xc = x - mu
    var = jnp.mean(xc * xc, axis=-1, keepdims=True)
    return xc * lax.rsqrt(var + LN_EPS) * g + b


def _rope_table_kernel(pos_ref, inv_ref, cos64_ref, sin64_ref, cos32_ref, sin32_ref):
    ang = pos_ref[...].astype(F32) * inv_ref[...]
    lane = lax.broadcasted_iota(jnp.int32, ang.shape, 1)
    c, s = jnp.cos(ang), jnp.sin(ang)

    def tile64(a):
        a = jnp.where(lane < 32, a, pltpu.roll(a, 32, 1))
        return jnp.where(lane < 64, a, pltpu.roll(a, 64, 1))

    def tile32(a):
        a = pltpu.roll(a, 96, 1)
        a = jnp.where(lane < 16, a, pltpu.roll(a, 16, 1))
        a = jnp.where(lane < 32, a, pltpu.roll(a, 32, 1))
        return jnp.where(lane < 64, a, pltpu.roll(a, 64, 1))

    cos64_ref[...] = tile64(c)
    sin64_ref[...] = jnp.where(lane % 64 < 32, -1.0, 1.0) * tile64(s)
    cos32_ref[...] = tile32(c)
    sin32_ref[...] = jnp.where(lane % 32 < 16, -1.0, 1.0) * tile32(s)


def _rope_tables(pos_col, inv_row):
    t = pos_col.shape[0]
    tm = 1024
    row = pl.BlockSpec((tm, LANES), lambda i: (i, 0))
    return pl.pallas_call(
        _rope_table_kernel,
        grid=(t // tm,),
        in_specs=[pl.BlockSpec((tm, 1), lambda i: (i, 0)),
                  pl.BlockSpec((1, LANES), lambda i: (0, 0))],
        out_specs=[row] * 4,
        out_shape=[jax.ShapeDtypeStruct((t, LANES), F32)] * 4,
        name="rope_tables",
    )(pos_col, inv_row)


def _rope_apply(h, cos, sin_signed, half):
    lane = lax.broadcasted_iota(jnp.int32, h.shape, 1)
    partner = jnp.where(lane % (2 * half) < half,
                        pltpu.roll(h, LANES - half, 1), pltpu.roll(h, half, 1))
    return h * cos + partner * sin_signed


def _in_proj_kernel(apply_ln, *refs):
    if apply_ln:
        (x_ref, lng_ref, lnb_ref, w_ref, cos64_ref, sin64_ref, cos32_ref, sin32_ref,
         kig_ref, kib_ref, xn_ref, *outs) = refs
        xn = _ln(x_ref[...], lng_ref[...], lnb_ref[...])
        xn_ref[...] = xn
        xb = xn.astype(BF16)
    else:
        (x_ref, w_ref, cos64_ref, sin64_ref, cos32_ref, sin32_ref,
         kig_ref, kib_ref, *outs) = refs
        xb = x_ref[...]
    out = dict(zip([n for n, _ in EXT_SEGS], outs))
    cos64, sin64 = cos64_ref[...], sin64_ref[...]
    cos32, sin32 = cos32_ref[...], sin32_ref[...]
    chunks = [(name, j) for name, width in EXT_SEGS for j in range(width // LANES)]
    mxu_n = 2 * LANES
    for c0 in range(0, len(chunks), mxu_n // LANES):
        group = chunks[c0:c0 + mxu_n // LANES]
        hh = jnp.dot(xb, w_ref[:, c0 * LANES:(c0 + len(group)) * LANES], preferred_element_type=F32)
        for k, (name, j) in enumerate(group):
            h = hh[:, k * LANES:(k + 1) * LANES]
            if name in ("qa", "qb"):
                h = _rope_apply(h, cos64, sin64, HEAD_DIM // 2) * (HEAD_DIM ** -0.5 * LOG2E)
            elif name in ("ka", "kb"):
                h = _rope_apply(h, cos64, sin64, HEAD_DIM // 2)
            elif name == "qi":
                h = _rope_apply(h, cos32, sin32, IDX_DIM // 2)
            elif name == "ki":
                h = _rope_apply(_ln(h, kig_ref[...], kib_ref[...]), cos32, sin32, IDX_DIM // 2)
            elif name == "wi":
                h = h * (IDX_HEADS ** -0.5 * IDX_DIM ** -0.5)
            if name == "vt":
                h = jnp.where(lax.broadcasted_iota(jnp.int32, h.shape, 1) < HEAD_DIM, h, 1.0)
                for r in range(h.shape[0] // SUB):
                    out[name][r] = h[r * SUB:(r + 1) * SUB].T.astype(BF16)
            else:
                out[name][:, j * LANES:(j + 1) * LANES] = h.astype(out[name].dtype)


def _in_proj(x, w_ext, tables, kig, kib, ln=None):
    t = x.shape[0]
    tm = 1024
    row = lambda w: pl.BlockSpec((tm, w), lambda i: (i, 0))
    const = lambda shape: pl.BlockSpec(shape, lambda i: (0, 0))
    in_specs = [row(D_MODEL)]
    args = [x]
    if ln is not None:
        in_specs += [const((1, D_MODEL))] * 2
        args += list(ln)
    in_specs += [const((D_MODEL, EXT_W))] + [row(LANES)] * 4 + [const((1, LANES))] * 2
    args += [w_ext, *tables, kig, kib]
    out_specs, out_shape = [], []
    if ln is not None:
        out_specs.append(row(D_MODEL))
        out_shape.append(jax.ShapeDtypeStruct((t, D_MODEL), F32))
    for name, width in EXT_SEGS:
        if name == "vt":
            out_specs.append(pl.BlockSpec((tm // SUB, LANES, SUB), lambda i: (i, 0, 0)))
            out_shape.append(jax.ShapeDtypeStruct((t // SUB, LANES, SUB), BF16))
            continue
        out_specs.append(row(width))
        out_shape.append(jax.ShapeDtypeStruct((t, width), F32 if name == "wi" else BF16))
    return pl.pallas_call(
        functools.partial(_in_proj_kernel, ln is not None),
        grid=(t // tm,),
        in_specs=in_specs, out_specs=out_specs, out_shape=out_shape,
        compiler_params=pltpu.CompilerParams(vmem_limit_bytes=VMEM_LIMIT),
        name="in_proj",
    )(*args)


def _swa_kernel(q_ref, kc_ref, kp_ref, vc_ref, vp_ref, sink_ref, o_ref):
    n = pl.program_id(1)
    row = lax.broadcasted_iota(jnp.int32, (LANES, BLOCK), 0)
    head = lambda a, h: a[:, h * LANES:(h + 1) * LANES]
    group = SWA_Q_HEADS // SWA_KV_HEADS
    swap = lambda a: pltpu.roll(a, HEAD_DIM, 0)
    kidx = lax.broadcasted_iota(jnp.int32, (2 * BLOCK, BLOCK), 0)
    diff = BLOCK + lax.broadcasted_iota(jnp.int32, (2 * BLOCK, BLOCK), 1) - kidx
    band = (diff >= 0) & (diff < WINDOW)
    sink = sink_ref[...] * LOG2E
    for i in range(SWA_STEP):
        rows = slice(i * BLOCK, (i + 1) * BLOCK)
        q = q_ref[0, rows, :].astype(F32)
        cols = []
        for j in range(SWA_Q_HEADS // 2):
            qt = head(q, j).T
            kv = (2 * j) // group
            keep = (row < HEAD_DIM) if kv == 0 else (row >= HEAD_DIM)
            even, odd = (qt, swap(qt)) if kv == 0 else (swap(qt), qt)
            cols += [jnp.where(keep, even, 0.0), jnp.where(keep, odd, 0.0)]
        qt = jnp.concatenate(cols, axis=1).astype(BF16)
        if i == 0:
            kk = jnp.concatenate([kp_ref[0], kc_ref[0, rows, :]], 0)
            vv = jnp.concatenate([vp_ref[0], vc_ref[0, rows, :]], 0)
            first_key = jnp.where(n > 0, 0, BLOCK)
            mask = band & (kidx >= first_key)
        else:
            kk = kc_ref[0, (i - 1) * BLOCK:(i + 1) * BLOCK, :]
            vv = vc_ref[0, (i - 1) * BLOCK:(i + 1) * BLOCK, :]
            mask = band
        s = jnp.dot(kk, qt, preferred_element_type=F32)
        s = jnp.concatenate([jnp.where(mask, head(s, h), NEG) for h in range(SWA_Q_HEADS)], axis=1)
        m = jnp.maximum(jnp.max(s, axis=0, keepdims=True), sink)
        p = jnp.exp2(s - m)
        r = 1.0 / (jnp.sum(p, axis=0, keepdims=True) + jnp.exp2(sink - m))
        vt = vv.astype(F32).T.astype(BF16)
        ot = jnp.dot(vt, p.astype(BF16), preferred_element_type=F32) * r
        for j in range(SWA_Q_HEADS // 2):
            a, b = head(ot, 2 * j), head(ot, 2 * j + 1)
            pair = jnp.where(row < HEAD_DIM, a, swap(b)) if (2 * j) // group == 0 else \
                jnp.where(row < HEAD_DIM, swap(a), b)
            o_ref[0, rows, j * LANES:(j + 1) * LANES] = pair.T.astype(o_ref.dtype)


def _swa(qa, ka, va, sink_row):
    b, s, _ = qa.shape
    tq = SWA_STEP * BLOCK
    cur = lambda w: pl.BlockSpec((1, tq, w), lambda bi, n: (bi, n, 0))
    prev = lambda w: pl.BlockSpec((1, BLOCK, w), lambda bi, n: (bi, jnp.maximum(n * SWA_STEP - 1, 0), 0))
    return pl.pallas_call(
        _swa_kernel,
        grid=(b, s // tq),
        in_specs=[cur(QA_W), cur(LANES), prev(LANES), cur(LANES), prev(LANES),
                  pl.BlockSpec((1, SWA_Q_HEADS * LANES), lambda bi, n: (0, 0))],
        out_specs=cur(QA_W),
        out_shape=jax.ShapeDtypeStruct((b, s, QA_W), BF16),
        name="swa_attn",
    )(qa, ka, ka, va, va, sink_row)


def _tree_sum(parts):
    while len(parts) > 1:
        parts = [a + b for a, b in zip(parts[::2], parts[1::2])] + ([parts[-1]] if len(parts) % 2 else [])
    return parts[0]


def _pack_digits(ukey, shift, bits):
    word = None
    for f in range(FIELDS):
        k = ukey[f * PACK:(f + 1) * PACK]
        pos = FIELD_BITS * f
        k = srl(k, shift - pos) if shift >= pos else k << (pos - shift)
        k = k & (((1 << bits) - 1) << pos)
        word = k if word is None else word | k
    return word | GUARDS


def _field_sum(acc):
    tot = _tree_sum([srl(acc, FIELD_BITS * f) & 0xFF for f in range(FIELDS)])
    return jnp.sum(tot, axis=0, keepdims=True)


def _dsa_kernel(qi_ref, wi_ref, qb_ref, ki_ref, kb_ref, vt_ref, o_ref,
                qim_ref, qs_ref, key_ref, dig_ref, act_ref, live_ref, sel_ref, s_ref, oacc_ref):
    n = pl.program_id(1)
    nu = n // (UNIT // BLOCK) + 1
    lane = lax.broadcasted_iota(jnp.int32, (BLOCK, LANES), 1)
    sub_rows = lambda u, j: pl.ds(pl.multiple_of(u * UNIT + j * SUB, SUB), SUB)
    pack_rows = lambda u, j: pl.ds(pl.multiple_of((u * UNIT + j * SUB) // FIELDS, PACK), PACK)
    pack_unit = lambda u: pl.ds(pl.multiple_of(u * (UNIT // FIELDS), UNIT // FIELDS), UNIT // FIELDS)
    key_pos = lambda u, j: u * UNIT + j * SUB + lax.broadcasted_iota(jnp.int32, (SUB, LANES), 0)
    causal = lambda u, j: key_pos(u, j) <= n * BLOCK + lax.broadcasted_iota(jnp.int32, (SUB, LANES), 1)
    head = lambda a, h: a[:, h * LANES:(h + 1) * LANES]
    words = lambda a: a.reshape(a.shape[0] // 8, 8, LANES)

    def each_unit(one, init):
        carry = lax.fori_loop(0, nu // 2, lambda i, c: one(2 * i + 1, one(2 * i, c)), init)
        return lax.cond(nu % 2 == 1, lambda: one(nu - 1, carry), lambda: carry)

    qi = qi_ref[0].astype(F32)
    for h in range(IDX_HEADS):
        per = LANES // IDX_DIM
        qim_ref[:, h * BLOCK:(h + 1) * BLOCK] = jnp.where(
            lane // IDX_DIM == h % per, head(qi, h // per), 0.0).T.astype(BF16)
    qb = qb_ref[0].astype(F32)
    for h in range(DSA_Q_HEADS):
        keep = (lane < HEAD_DIM) if h % 2 == 0 else (lane >= HEAD_DIM)
        qs_ref[:, h * BLOCK:(h + 1) * BLOCK] = jnp.where(keep, head(qb, h // 2), 0.0).T.astype(BF16)
    wt = wi_ref[0].T
    w_rows = [wt[h:h + 1, :] for h in range(IDX_HEADS)]

    shifts = [sum(DIGIT_BITS[i + 1:]) for i in range(len(DIGIT_BITS))]

    def idx_unit(u, carry):
        for j in range(UNIT // SUB):
            rows = sub_rows(u, j)
            sc = jnp.dot(ki_ref[0, rows, :], qim_ref[...], preferred_element_type=F32)
            acc = jnp.maximum(head(sc, 0), 0.0) * w_rows[0]
            for h in range(1, IDX_HEADS):
                acc = acc + jnp.maximum(head(sc, h), 0.0) * w_rows[h]
            acc = jnp.where(acc == 0.0, 0.0, acc)
            idx = jnp.where(causal(u, j), acc, NEG)
            bits = lax.bitcast_convert_type(idx, I32)
            key = bits ^ ((bits >> 31) & INT_MAX)
            key_ref[rows, :] = key
            ukey = key ^ INT_MIN
            for p, (shift, nbits) in enumerate(zip(shifts, DIGIT_BITS)):
                dig_ref[p, pack_rows(u, j), :] = _pack_digits(ukey, shift, nbits)
        return carry

    lax.fori_loop(0, nu, idx_unit, 0)

    def count_ge(cand):
        crep = jnp.broadcast_to(cand * ONES, (8, LANES))

        def one(u, acc):
            a = words(act_ref[pack_unit(u), :])
            t = srl(a - crep[None], FIELD_BITS - 1) & ONES
            return acc + _tree_sum([t[i] for i in range(t.shape[0])])

        return _field_sum(each_unit(one, jnp.zeros((8, LANES), I32)))

    def digit_search(nbits, need, live):
        def bit(i, carry):
            d, above, atleast = carry
            cand = d | (jnp.int32(1) << (nbits - 1 - i))
            cnt = count_ge(cand)
            ok = cnt >= need
            return jnp.where(ok, cand, d), jnp.where(ok, above, cnt), jnp.where(ok, cnt, atleast)

        zero = jnp.zeros((1, LANES), I32)
        return lax.fori_loop(0, nbits, bit, (zero, zero, live))

    def narrow(p, digit, first):
        want = jnp.broadcast_to((digit * ONES) | GUARDS, (8, LANES))[None]

        def one(u, carry):
            rows = pack_unit(u)
            e = words(dig_ref[p, rows, :]) ^ want
            nonzero = (((e & LOW7) + LOW7) | e) & GUARDS
            zero = nonzero ^ GUARDS
            match = zero | (zero - srl(zero, FIELD_BITS - 1))
            if not first:
                match = match & words(live_ref[rows, :])
            live_ref[rows, :] = match.reshape(UNIT // FIELDS, LANES)
            nxt = (words(dig_ref[p + 1, rows, :]) & match) | GUARDS
            act_ref[rows, :] = nxt.reshape(UNIT // FIELDS, LANES)
            return carry

        each_unit(one, 0)

    def write_sel(keep):
        def one(u, acc):
            for j in range(UNIT // SUB):
                rows = sub_rows(u, j)
                m = keep(key_ref[rows, :], key_pos(u, j))
                sel_ref[rows, :] = (m & causal(u, j)).astype(F32)
                acc = acc + jnp.sum(m.astype(I32).reshape(SUB // 8, 8, LANES), axis=0)
            return acc

        return jnp.sum(each_unit(one, jnp.zeros((8, LANES), I32)), axis=0, keepdims=True)

    @pl.when(n * BLOCK < TOPK_MAX)
    def _():
        write_sel(lambda k, kp: k == k)

    @pl.when(n * BLOCK >= TOPK_MAX)
    def _():
        def copy_first(u, carry):
            act_ref[pack_unit(u), :] = dig_ref[0, pack_unit(u), :]
            return carry

        each_unit(copy_first, 0)
        need = jnp.full((1, LANES), TOPK_MAX, I32)
        live = jnp.zeros((1, LANES), I32) + nu * UNIT
        ukey_thr = jnp.zeros((1, LANES), I32)
        last = len(DIGIT_BITS) - 1
        for p, nbits in enumerate(DIGIT_BITS[:last]):
            d, above, atleast = digit_search(nbits, need, live)
            ukey_thr = (ukey_thr << nbits) | d
            need, live = need - above, atleast - above
            if p + 1 < last:
                narrow(p, d, first=(p == 0))

        def last_digit():
            narrow(last - 1, d, first=False)
            return digit_search(DIGIT_BITS[last], need, live)[:2]

        resolved = jnp.min((need == live).astype(I32)) == 1
        d_last, above = lax.cond(resolved, lambda: (jnp.zeros_like(need), jnp.zeros_like(need)), last_digit)
        ukey_thr = (ukey_thr << DIGIT_BITS[last]) | d_last
        need = need - above
        thr = ukey_thr ^ INT_MIN
        nge = write_sel(lambda k, kp: k >= thr)

        @pl.when(jnp.max(nge) > TOPK_MAX)
        def _():
            missing = need
            pos_bits = (ki_ref.shape[1] - 1).bit_length()

            def tie_positions(u, carry):
                for j in range(UNIT // SUB):
                    rows = sub_rows(u, j)
                    tie = key_ref[rows, :] == thr
                    sel_ref[rows, :] = jnp.where(tie, key_pos(u, j), INT_MAX).astype(F32)
                return carry

            each_unit(tie_positions, 0)

            def count_below(cj):
                cf = cj.astype(F32)

                def one(u, acc):
                    for j in range(UNIT // SUB):
                        m = (sel_ref[sub_rows(u, j), :] < cf).astype(I32)
                        acc = acc + jnp.sum(m.reshape(SUB // 8, 8, LANES), axis=0)
                    return acc

                return jnp.sum(each_unit(one, jnp.zeros((8, LANES), I32)), axis=0, keepdims=True)

            def jbit(i, jcut):
                cj = jcut | (jnp.int32(1) << (pos_bits - 1 - i))
                return jnp.where(count_below(cj) < missing, cj, jcut)

            jcut = lax.fori_loop(0, pos_bits, jbit, jnp.zeros((1, LANES), I32))
            jcut = jnp.where(nge > TOPK_MAX, jcut, INT_MAX)
            write_sel(lambda k, kp: (k > thr) | ((k == thr) & (kp <= jcut)))

    width = DSA_Q_HEADS * LANES
    oacc_ref[...] = jnp.zeros(oacc_ref.shape, F32)

    def scores(u, j):
        rows = sub_rows(u, j)
        sel = sel_ref[rows, :] > 0.0
        s = jnp.dot(kb_ref[0, rows, :], qs_ref[...], preferred_element_type=F32)
        tops = []
        for h in range(DSA_Q_HEADS):
            sh = jnp.where(sel, head(s, h), -jnp.inf)
            s_ref[j, :, h * LANES:(h + 1) * LANES] = sh
            tops.append(jnp.max(sh.reshape(SUB // 8, 8, LANES), axis=0))
        return jnp.max(jnp.concatenate(tops, axis=1), axis=0, keepdims=True)

    def accumulate(u, j, m_run, top):
        m_new = jnp.maximum(m_run, top)
        p = jnp.exp2(s_ref[j] - m_new)
        oacc_ref[...] = oacc_ref[...] * jnp.exp2(m_run - m_new) + jnp.dot(
            vt_ref[u * (UNIT // SUB) + j], p.astype(BF16), preferred_element_type=F32)
        return m_new

    def attn_unit(u, carry):
        m_run, top0 = carry
        top1 = scores(u, 1)
        m_run = accumulate(u, 0, m_run, top0)
        top0 = scores(jnp.minimum(u + 1, nu - 1), 0)
        m_run = accumulate(u, 1, m_run, top1)
        return m_run, top0

    lax.fori_loop(0, nu, attn_unit, (jnp.full((1, width), NEG, F32), scores(0, 0)))
    o = oacc_ref[...]
    o = o * (1.0 / o[HEAD_DIM:HEAD_DIM + 1, :])
    row = lax.broadcasted_iota(jnp.int32, (LANES, BLOCK), 0)
    for j in range(DSA_Q_HEADS // 2):
        pair = jnp.where(row < HEAD_DIM, head(o, 2 * j), pltpu.roll(head(o, 2 * j + 1), HEAD_DIM, 0))
        o_ref[0, :, j * LANES:(j + 1) * LANES] = pair.T.astype(o_ref.dtype)


def _dsa(qi, wi, qb, ki, kb, vt):
    b, s, _ = qb.shape
    assert s % UNIT == 0 and s // 4 >= TOPK_MAX, "sequence must be a multiple of UNIT with k_top == TOPK_MAX"
    assert s // FIELDS // 8 < 2 ** FIELD_BITS, "per-field counters of the threshold search would overflow"
    cur = lambda w: pl.BlockSpec((1, BLOCK, w), lambda bi, n: (bi, n, 0))
    seq = pl.BlockSpec((1, s, LANES), lambda bi, n: (bi, 0, 0))
    rows = DSA_Q_HEADS * BLOCK
    return pl.pallas_call(
        _dsa_kernel,
        grid=(b, s // BLOCK),
        in_specs=[cur(QI_W), cur(LANES), cur(QB_W), seq, seq,
                  pl.BlockSpec((s // SUB, LANES, SUB), lambda bi, n: (bi, 0, 0))],
        out_specs=cur(QB_W),
        out_shape=jax.ShapeDtypeStruct((b, s, QB_W), BF16),
        scratch_shapes=[
            pltpu.VMEM((LANES, IDX_HEADS * BLOCK), BF16),
            pltpu.VMEM((LANES, rows), BF16),
            pltpu.VMEM((s, LANES), I32),
            pltpu.VMEM((len(DIGIT_BITS), s // FIELDS, LANES), I32),
            pltpu.VMEM((s // FIELDS, LANES), I32),
            pltpu.VMEM((s // FIELDS, LANES), I32),
            pltpu.VMEM((s, LANES), F32),
            pltpu.VMEM((UNIT // SUB, SUB, rows), F32),
            pltpu.VMEM((LANES, rows), F32),
        ],
        compiler_params=pltpu.CompilerParams(vmem_limit_bytes=VMEM_LIMIT),
        name="dsa_attn",
    )(qi, wi, qb, ki, kb, vt)


def _ffn_kernel(alpha, d_ff, oa_ref, ob_ref, x_ref, p_ref, wo_ref, g1_ref, b1_ref,
                wgu_ref, wd_ref, wpg_ref, wpp_ref, g_ref, b_ref, y_ref, yb_ref, h_ref):
    mix = (jnp.dot(oa_ref[...], wo_ref[:QA_W, :], preferred_element_type=F32)
           + jnp.dot(ob_ref[...], wo_ref[QA_W:, :], preferred_element_type=F32))
    x1 = _ln(alpha * x_ref[...] + mix, g1_ref[...], b1_ref[...])
    xb = x1.astype(BF16)
    fc = 2 * LANES
    for j in range(d_ff // fc):
        g = jnp.dot(xb, wgu_ref[:, j * fc:(j + 1) * fc], preferred_element_type=F32)
        u = jnp.dot(xb, wgu_ref[:, d_ff + j * fc:d_ff + (j + 1) * fc], preferred_element_type=F32)
        h_ref[:, j * fc:(j + 1) * fc] = (g * jax.nn.sigmoid(g) * u).astype(BF16)
    ffn = jnp.dot(h_ref[...], wd_ref[...], preferred_element_type=F32)
    gate = jax.nn.sigmoid(jnp.dot(xb, wpg_ref[...], preferred_element_type=F32))
    ple = jnp.dot(p_ref[...].astype(BF16), wpp_ref[...], preferred_element_type=F32) * gate
    y = _ln(alpha * x1 + ffn + ple, g_ref[...], b_ref[...])
    y_ref[...] = y
    yb_ref[...] = y.astype(BF16)


def _ffn(oa, ob, x, p, wo, g1, b1, wgu, wd, wpg, wpp, g, b, alpha):
    t = x.shape[0]
    d_ff = wd.shape[0]
    tm = 512
    row = lambda w: pl.BlockSpec((tm, w), lambda i: (i, 0))
    const = lambda shape: pl.BlockSpec(shape, lambda i: (0, 0), pipeline_mode=pl.Buffered(1))
    vec = const((1, D_MODEL))
    return pl.pallas_call(
        functools.partial(_ffn_kernel, alpha, d_ff),
        grid=(t // tm,),
        in_specs=[row(QA_W), row(QB_W), row(D_MODEL), row(PLE_DIM), const(wo.shape), vec, vec,
                  const(wgu.shape), const(wd.shape), const(wpg.shape), const(wpp.shape), vec, vec],
        out_specs=[row(D_MODEL), row(D_MODEL)],
        out_shape=[jax.ShapeDtypeStruct((t, D_MODEL), F32), jax.ShapeDtypeStruct((t, D_MODEL), BF16)],
        scratch_shapes=[pltpu.VMEM((tm, d_ff), BF16)],
        compiler_params=pltpu.CompilerParams(vmem_limit_bytes=VMEM_LIMIT),
        name="mix_ffn_ln",
    )(oa, ob, x, p, wo, g1, b1, wgu, wd, wpg, wpp, g, b)


def _extend_w_in(w):
    cuts = [0]
    for sz in IN_SIZES:
        cuts.append(cuts[-1] + sz)
    qa, ka, va, qb, kb, vb, qi, ki, wi = [w[:, cuts[i]:cuts[i + 1]] for i in range(len(IN_SIZES))]
    pad = jnp.zeros((w.shape[0], LANES - IDX_HEADS), w.dtype)
    ext = jnp.concatenate([qa, ka, va, qb, kb, kb, vb, vb, qi, ki, ki, ki, ki, wi, pad], axis=1)
    return ext.astype(BF16)


def kernel(x, p, positions, ln_in_g, ln_in_b, w_in, attn_sinks, idx_k_g, idx_k_b, w_o, ln1_g, ln1_b,
           w_gu, w_down, w_pg, w_pp, ln2_g, ln2_b):
    bsz, seq, d = x.shape
    depth = w_in.shape[0]
    t = bsz * seq
    alpha = (2 * depth) ** 0.25
    row = lambda v: v.reshape(1, -1).astype(F32)

    inv64 = ROPE_THETA ** (-jnp.arange(HEAD_DIM // 2, dtype=F32) / (HEAD_DIM // 2))
    inv32 = ROPE_THETA ** (-jnp.arange(IDX_DIM // 2, dtype=F32) / (IDX_DIM // 2))
    inv_row = jnp.concatenate([inv64, inv32, jnp.zeros((LANES - 48,), F32)]).reshape(1, LANES)
    tables = _rope_tables(positions.reshape(t, 1), inv_row)

    xf = x.reshape(t, d)
    xb = None
    for i in range(depth):
        w_ext = _extend_w_in(w_in[i])
        kig = jnp.tile(idx_k_g[i], LANES // IDX_DIM).reshape(1, LANES)
        kib = jnp.tile(idx_k_b[i], LANES // IDX_DIM).reshape(1, LANES)
        if i == 0:
            xf, *segs = _in_proj(xf, w_ext, tables, kig, kib, ln=(row(ln_in_g), row(ln_in_b)))
        else:
            segs = _in_proj(xb, w_ext, tables, kig, kib)
        qa, ka, va, qb, kb, vt, qi, ki, wi = [a if a.ndim == 3 else a.reshape(bsz, seq, -1) for a in segs]
        sink_row = jnp.repeat(attn_sinks[i].astype(F32), BLOCK).reshape(1, -1)
        oa = _swa(qa, ka, va, sink_row).reshape(t, QA_W)
        ob = _dsa(qi, wi, qb, ki, kb, vt).reshape(t, QB_W)
        xf, xb = _ffn(oa, ob, xf, p[i].reshape(t, PLE_DIM), w_o[i].astype(BF16), row(ln1_g[i]), row(ln1_b[i]),
                      w_gu[i].astype(BF16), w_down[i].astype(BF16), w_pg[i].astype(BF16), w_pp[i].astype(BF16),
                      row(ln2_g[i]), row(ln2_b[i]), alpha)
    return xf.reshape(bsz, seq, d)
```
